```python
import math
import jax, jax.numpy as jnp
from jax import lax
import numpy as np

D_MODEL = 2048
BATCH = 1
SEQ = 8192
DEPTH = 2
DEC_BATCH = 128
DEC_SEQ = 1
PAST_LEN = 8192
PAGE_SIZE = 128

N_EVEN = (DEPTH + 1) // 2
N_ODD = DEPTH // 2
REL_BUCKETS = 32
REL_MAX_EXACT = 16
REL_MAX_DIST = 128
REL_HEADS = 16
POOL_WINDOWS = (2, 4, 8, 16)
POOL_WIDTH = D_MODEL // 2
POOL_GROUP = POOL_WIDTH // len(POOL_WINDOWS)
POOL_STATE = max(POOL_WINDOWS) - 1
WINDOW = 128
SWA_HEADS = 16
SWA_KV_HEADS = 4
SWA_HEAD_DIM = 64
SWA_GROUP = SWA_HEADS // SWA_KV_HEADS
SWA_Q_W = SWA_HEADS * SWA_HEAD_DIM
SWA_KV_W = SWA_KV_HEADS * SWA_HEAD_DIM
EVEN_IN_W = POOL_WIDTH + SWA_Q_W + 2 * SWA_KV_W
DIFF_HEADS = 8
DIFF_KV_HEADS = 4
DIFF_GROUP = DIFF_HEADS // DIFF_KV_HEADS
DIFF_HEAD_DIM = 128
DIFF_V_DIM = 2 * DIFF_HEAD_DIM
DIFF_Q_W = DIFF_HEADS * 2 * DIFF_HEAD_DIM
DIFF_K_W = DIFF_KV_HEADS * 2 * DIFF_HEAD_DIM
DIFF_V_W = DIFF_KV_HEADS * DIFF_V_DIM
DIFF_OUT_W = DIFF_HEADS * DIFF_V_DIM
ODD_IN_W = DIFF_Q_W + DIFF_K_W + DIFF_V_W
MEM_LEN = 256
MEM_HEADS = 4
MEM_HEAD_DIM = 128
MEM_W = MEM_HEADS * MEM_HEAD_DIM
D_FF = 5632
N_EXPERTS = 8
TOP_K = 2
EXPERT_FF = 7168
Q_BLOCK = 128
EPS = 1e-6

kernel_name = 'hybrid_pool_swa_diffattn_moe_decoder_step'

F32 = jnp.float32


def rmsnorm(x, gain):
    x32 = x.astype(F32)
    y = x32 * lax.rsqrt(jnp.mean(x32 * x32, axis=-1, keepdims=True) + EPS)
    return (y * gain.astype(F32)).astype(x.dtype)


def rel_position_bias(dist, table):
    n = jnp.maximum(dist, 0)
    log_ratio = jnp.log(jnp.maximum(n, REL_MAX_EXACT).astype(F32) / REL_MAX_EXACT)
    large = REL_MAX_EXACT + (log_ratio / math.log(REL_MAX_DIST / REL_MAX_EXACT) * (REL_BUCKETS - REL_MAX_EXACT)).astype(jnp.int32)
    bucket = jnp.where(n < REL_MAX_EXACT, n, jnp.minimum(large, REL_BUCKETS - 1))
    return table[bucket].astype(F32)


def multiscale_pool_mixer(u_ext, n_new, pool_map, pool_scale):
    b, l, _ = u_ext.shape
    cs = jnp.cumsum(u_ext.astype(F32), axis=1)
    cs = jnp.concatenate([jnp.zeros((b, 1, POOL_WIDTH), F32), cs], axis=1)
    idx = jnp.arange(l - n_new, l)
    means = []
    for g, w in enumerate(POOL_WINDOWS):
        lo = jnp.maximum(idx + 1 - w, 0)
        cnt = (idx + 1 - lo).astype(F32)
        cg = cs[:, :, g * POOL_GROUP:(g + 1) * POOL_GROUP]
        means.append((cg[:, idx + 1] - cg[:, lo]) / cnt[None, :, None])
    pooled = jnp.concatenate(means, axis=-1) - u_ext[:, l - n_new:].astype(F32)
    mixed = jnp.einsum('blgc,gcd->blgd', pooled.reshape(b, n_new, len(POOL_WINDOWS), POOL_GROUP), pool_map.astype(F32))
    return (mixed.reshape(b, n_new, POOL_WIDTH) * pool_scale.astype(F32)).astype(u_ext.dtype)


def swa_sink_attention(q, k, v, key_ok, sinks, rel_table):
    b, nb, lq = q.shape[:3]
    lk = k.shape[2]
    dist = WINDOW + jnp.arange(lq)[:, None] - jnp.arange(lk)[None, :]
    bias = rel_position_bias(dist, rel_table).reshape(lq, lk, SWA_KV_HEADS, SWA_GROUP).transpose(2, 3, 0, 1)
    band = (dist >= 0) & (dist <= WINDOW)
    mask = band[None, None, None, None] & key_ok[None, :, None, None, None, :]
    qg = q.reshape(b, nb, lq, SWA_KV_HEADS, SWA_GROUP, SWA_HEAD_DIM)
    s = jnp.einsum('bnqhgd,bnkhd->bnhgqk', qg, k).astype(F32) * SWA_HEAD_DIM ** -0.5 + bias
    s = jnp.where(mask, s, -jnp.inf)
    sink = sinks.astype(F32).reshape(1, 1, SWA_KV_HEADS, SWA_GROUP, 1, 1)
    m = jnp.maximum(jnp.max(s, axis=-1, keepdims=True), sink)
    p = jnp.exp(s - m)
    p = p / (jnp.sum(p, axis=-1, keepdims=True) + jnp.exp(sink - m))
    o = jnp.einsum('bnhgqk,bnkhd->bnqhgd', p, v.astype(F32))
    return o.reshape(b, nb * lq, SWA_Q_W).astype(q.dtype)


def split_even(u):
    b, l, _ = u.shape
    o1 = POOL_WIDTH
    o2 = o1 + SWA_Q_W
    o3 = o2 + SWA_KV_W
    q = u[..., o1:o2].reshape(b, l, SWA_HEADS, SWA_HEAD_DIM)
    k = u[..., o2:o3].reshape(b, l, SWA_KV_HEADS, SWA_HEAD_DIM)
    v = u[..., o3:].reshape(b, l, SWA_KV_HEADS, SWA_HEAD_DIM)
    return u[..., :o1], q, k, v


def even_mixer_prompt(h, w_in, pool_map, pool_scale, sinks, w_out, rel_table):
    u_pool, q, k, v = split_even(h @ w_in)
    b, l = h.shape[:2]
    nb = l // WINDOW
    a_out = multiscale_pool_mixer(u_pool, l, pool_map, pool_scale)

    def band(t):
        t = jnp.concatenate([jnp.zeros((b, WINDOW) + t.shape[2:], t.dtype), t], axis=1)
        t = t.reshape((b, nb + 1, WINDOW) + t.shape[2:])
        return jnp.concatenate([t[:, :-1], t[:, 1:]], axis=2)

    key_ok = (jnp.arange(nb)[:, None] > 0) | (jnp.arange(2 * WINDOW)[None, :] >= WINDOW)
    b_out = swa_sink_attention(q.reshape(b, nb, WINDOW, SWA_HEADS, SWA_HEAD_DIM), band(k), band(v), key_ok, sinks, rel_table)
    y = jnp.concatenate([a_out, b_out], axis=-1) @ w_out
    return y, u_pool[:, l - POOL_STATE:], k[:, l - WINDOW:], v[:, l - WINDOW:]


def even_mixer_sample(h, pool_buf, k_buf, v_buf, w_in, pool_map, pool_scale, sinks, w_out, rel_table):
    u_pool, q, k, v = split_even(h @ w_in)
    l = h.shape[1]
    u_ext = jnp.concatenate([pool_buf, u_pool], axis=1)
    a_out = multiscale_pool_mixer(u_ext, l, pool_map, pool_scale)
    k_ext = jnp.concatenate([k_buf, k], axis=1)
    v_ext = jnp.concatenate([v_buf, v], axis=1)
    key_ok = jnp.ones((1, WINDOW + l), dtype=bool)
    b_out = swa_sink_attention(q[:, None], k_ext[:, None], v_ext[:, None], key_ok, sinks, rel_table)
    y = jnp.concatenate([a_out, b_out], axis=-1) @ w_out
    return y, u_ext[:, -POOL_STATE:], k_ext[:, -WINDOW:], v_ext[:, -WINDOW:]


def split_odd(u):
    b, l, _ = u.shape
    q = u[..., :DIFF_Q_W].reshape(b, l, DIFF_KV_HEADS, DIFF_GROUP, 2, DIFF_HEAD_DIM)
    k = u[..., DIFF_Q_W:DIFF_Q_W + DIFF_K_W].reshape(b, l, DIFF_KV_HEADS, 2, DIFF_HEAD_DIM)
    v = u[..., DIFF_Q_W + DIFF_K_W:].reshape(b, l, DIFF_KV_HEADS, DIFF_V_DIM)
    return q, k, v


def diff_bias(dist, rel_table):
    lq, lk = dist.shape
    bias = rel_position_bias(dist, rel_table).reshape(lq, lk, DIFF_KV_HEADS, DIFF_GROUP, 2).transpose(2, 3, 4, 0, 1)
    return jnp.where(dist >= 0, bias, -jnp.inf)


def diff_partial(q, k, v, bias):
    s = jnp.einsum('bqhgmd,bkhmd->bhgmqk', q, k).astype(F32) * DIFF_HEAD_DIM ** -0.5 + bias
    mx = jnp.max(s, axis=-1)
    p = jnp.exp(s - mx[..., None])
    return mx, jnp.sum(p, axis=-1), jnp.einsum('bhgmqk,bkhv->bhgmqv', p, v.astype(F32))


def merge_partial(a, c):
    m = jnp.maximum(a[0], c[0])
    sa = jnp.exp(a[0] - m)
    sc = jnp.exp(c[0] - m)
    return m, a[1] * sa + c[1] * sc, a[2] * sa[..., None] + c[2] * sc[..., None]


def diff_lambda_value(lp, lam_init):
    lp = lp.astype(F32)
    return jnp.exp(jnp.sum(lp[0] * lp[1])) - jnp.exp(jnp.sum(lp[2] * lp[3])) + lam_init


def diff_finish(stats, lam, lam_init, subln):
    _, denom, acc = stats
    o = acc / denom[..., None]
    o = o[:, :, :, 0] - lam * o[:, :, :, 1]
    o = o * lax.rsqrt(jnp.mean(o * o, axis=-1, keepdims=True) + EPS) * subln.astype(F32) * (1.0 - lam_init)
    b, _, _, lq, _ = o.shape
    return o.transpose(0, 3, 1, 2, 4).reshape(b, lq, DIFF_OUT_W)


def odd_mixer_prompt(h, w_qkv, lam_params, subln, w_out, lam_init, rel_table):
    q, k, v = split_odd(h @ w_qkv)
    b, l = h.shape[:2]
    lam = diff_lambda_value(lam_params, lam_init)
    kpos = jnp.arange(l)

    def block(i):
        qb = lax.dynamic_slice_in_dim(q, i * Q_BLOCK, Q_BLOCK, axis=1)
        dist = (i * Q_BLOCK + jnp.arange(Q_BLOCK))[:, None] - kpos[None, :]
        return diff_finish(diff_partial(qb, k, v, diff_bias(dist, rel_table)), lam, lam_init, subln)

    o = lax.map(block, jnp.arange(l // Q_BLOCK))
    o = o.transpose(1, 0, 2, 3).reshape(b, l, DIFF_OUT_W)
    return o.astype(h.dtype) @ w_out, k, v


def odd_mixer_sample(h, cache_k, cache_v, page_table, li, w_qkv, lam_params, subln, w_out, lam_init, rel_table):
    q, k, v = split_odd(h @ w_qkv)
    l = h.shape[1]
    n_pages = page_table.shape[1]
    qpos = n_pages * PAGE_SIZE + jnp.arange(l)
    lam = diff_lambda_value(lam_params, lam_init)
    stats = diff_partial(q, k, v, diff_bias(jnp.arange(l)[:, None] - jnp.arange(l)[None, :], rel_table))

    def page_step(carry, xs):
        p, phys = xs
        kp = cache_k[phys][:, :, li]
        vp = cache_v[phys][:, :, li]
        dist = qpos[:, None] - (p * PAGE_SIZE + jnp.arange(PAGE_SIZE))[None, :]
        return merge_partial(carry, diff_partial(q, kp, vp, diff_bias(dist, rel_table))), None

    stats, _ = lax.scan(page_step, stats, (jnp.arange(n_pages), page_table.T))
    o = diff_finish(stats, lam, lam_init, subln)
    return o.astype(h.dtype) @ w_out, k, v


def memory_cross_attention(h, mem_k, mem_v, w_q, w_o):
    b, l, _ = h.shape
    q = (h @ w_q).reshape(b, l, MEM_HEADS, MEM_HEAD_DIM)
    s = jnp.einsum('blhd,bmhd->bhlm', q, mem_k).astype(F32) * MEM_HEAD_DIM ** -0.5
    p = jax.nn.softmax(s, axis=-1)
    o = jnp.einsum('bhlm,bmhd->blhd', p, mem_v.astype(F32)).reshape(b, l, MEM_W)
    return o.astype(h.dtype) @ w_o


def swiglu(h, w_gu, w_down):
    g, u = jnp.split(h @ w_gu, 2, axis=-1)
    return (jax.nn.silu(g) * u) @ w_down


def moe_swiglu(h, w_router, w_gu, w_down):
    logits = (h @ w_router).astype(F32)
    top_v, top_i = lax.top_k(logits, TOP_K)
    gates = jax.nn.softmax(top_v, axis=-1)
    gate_full = jnp.sum(jax.nn.one_hot(top_i, N_EXPERTS, dtype=F32) * gates[..., None], axis=-2)
    out = jnp.zeros(h.shape, F32)
    for e in range(N_EXPERTS):
        out = out + swiglu(h, w_gu[e], w_down[e]).astype(F32) * gate_full[..., e:e + 1]
    return out.astype(h.dtype)


def setup_inputs(seed: int = 0) -> dict:
    key = jax.random.key(seed)
    ks = iter(jax.random.split(key, 48))

    def nrm(shape, scale=1.0):
        return jax.random.normal(next(ks), shape, F32) * scale

    n_pages = PAST_LEN // PAGE_SIZE
    n_used = DEC_BATCH * n_pages
    n_phys = n_used + n_used // 4
    page_table = jax.random.permutation(next(ks), n_phys)[:n_used].reshape(DEC_BATCH, n_pages).astype(jnp.int32)
    d_s = D_MODEL ** -0.5
    return {
        'x_prompt': nrm((BATCH, SEQ, D_MODEL)),
        'x_sample': nrm((DEC_BATCH, DEC_SEQ, D_MODEL)),
        'mem_prompt': nrm((BATCH, MEM_LEN, D_MODEL)),
        'state_pool': nrm((N_EVEN, DEC_BATCH, POOL_STATE, POOL_WIDTH)),
        'state_swa_k': nrm((N_EVEN, DEC_BATCH, WINDOW, SWA_KV_HEADS, SWA_HEAD_DIM)),
        'state_swa_v': nrm((N_EVEN, DEC_BATCH, WINDOW, SWA_KV_HEADS, SWA_HEAD_DIM)),
        'cache_diff_k': nrm((n_phys, PAGE_SIZE, N_ODD, DIFF_KV_HEADS, 2, DIFF_HEAD_DIM)),
        'cache_diff_v': nrm((n_phys, PAGE_SIZE, N_ODD, DIFF_KV_HEADS, DIFF_V_DIM)),
        'cache_mem_k': nrm((DEPTH, DEC_BATCH, MEM_LEN, MEM_HEADS, MEM_HEAD_DIM)),
        'cache_mem_v': nrm((DEPTH, DEC_BATCH, MEM_LEN, MEM_HEADS, MEM_HEAD_DIM)),
        'page_table': page_table,
        'rel_bias_table': nrm((REL_BUCKETS, REL_HEADS), 0.5),
        'norm_gain': 1.0 + nrm((DEPTH, 3, D_MODEL), 0.02),
        'final_gain': 1.0 + nrm((D_MODEL,), 0.02),
        'w_in_even': nrm((N_EVEN, D_MODEL, EVEN_IN_W), d_s),
        'pool_map': nrm((N_EVEN, len(POOL_WINDOWS), POOL_GROUP, POOL_GROUP), POOL_GROUP ** -0.5),
        'pool_scale': 1.0 + nrm((N_EVEN, POOL_WIDTH), 0.02),
        'attn_sinks': nrm((N_EVEN, SWA_HEADS), 0.5),
        'w_out_even': nrm((N_EVEN, D_MODEL, D_MODEL), d_s),
        'w_qkv_odd': nrm((N_ODD, D_MODEL, ODD_IN_W), d_s),
        'diff_lambda': nrm((N_ODD, 4, DIFF_HEAD_DIM), 0.1),
        'diff_subln': 1.0 + nrm((N_ODD, DIFF_V_DIM), 0.02),
        'w_out_odd': nrm((N_ODD, DIFF_OUT_W, D_MODEL), DIFF_OUT_W ** -0.5),
        'w_mem_q': nrm((DEPTH, D_MODEL, MEM_W), d_s),
        'w_mem_k': nrm((DEPTH, D_MODEL, MEM_W), d_s),
        'w_mem_v': nrm((DEPTH, D_MODEL, MEM_W), d_s),
        'w_mem_o': nrm((DEPTH, MEM_W, D_MODEL), MEM_W ** -0.5),
        'w_ffn_gu': nrm((N_EVEN, D_MODEL, 2 * D_FF), d_s),
        'w_ffn_down': nrm((N_EVEN, D_FF, D_MODEL), D_FF ** -0.5),
        'w_router': nrm((N_ODD, D_MODEL, N_EXPERTS), d_s),
        'w_exp_gu': nrm((N_ODD, N_EXPERTS, D_MODEL, 2 * EXPERT_FF), d_s),
        'w_exp_down': nrm((N_ODD, N_EXPERTS, EXPERT_FF, D_MODEL), EXPERT_FF ** -0.5),
    }


def reference(x_prompt, x_sample, mem_prompt, state_pool, state_swa_k, state_swa_v, cache_diff_k, cache_diff_v,
              cache_mem_k, cache_mem_v, page_table, rel_bias_table, norm_gain, final_gain, w_in_even, pool_map,
              pool_scale, attn_sinks, w_out_even, w_qkv_odd, diff_lambda, diff_subln, w_out_odd, w_mem_q, w_mem_k,
              w_mem_v, w_mem_o, w_ffn_gu, w_ffn_down, w_router, w_exp_gu, w_exp_down):
    hp, hs = x_prompt, x_sample
    bp = x_prompt.shape[0]
    pool_p, pool_s, swk_p, swv_p, swk_s, swv_s = [], [], [], [], [], []
    dk_p, dv_p, dk_s, dv_s, mk_p, mv_p = [], [], [], [], [], []
    for layer in range(DEPTH):
        g_mix, g_mem, g_ffn = norm_gain[layer, 0], norm_gain[layer, 1], norm_gain[layer, 2]
        if layer % 2 == 0:
            e = layer // 2
            y, st_pool, st_k, st_v = even_mixer_prompt(rmsnorm(hp, g_mix), w_in_even[e], pool_map[e], pool_scale[e],
                                                       attn_sinks[e], w_out_even[e], rel_bias_table)
            hp = hp + y
            pool_p.append(st_pool); swk_p.append(st_k); swv_p.append(st_v)
            y, st_pool, st_k, st_v = even_mixer_sample(rmsnorm(hs, g_mix), state_pool[e], state_swa_k[e], state_swa_v[e],
                                                       w_in_even[e], pool_map[e], pool_scale[e], attn_sinks[e],
                                                       w_out_even[e], rel_bias_table)
            hs = hs + y
            pool_s.append(st_pool); swk_s.append(st_k); swv_s.append(st_v)
        else:
            o = layer // 2
            lam_init = 0.8 - 0.6 * math.exp(-0.3 * layer)
            y, k_new, v_new = odd_mixer_prompt(rmsnorm(hp, g_mix), w_qkv_odd[o], diff_lambda[o], diff_subln[o],
                                               w_out_odd[o], lam_init, rel_bias_table)
            hp = hp + y
            dk_p.append(k_new); dv_p.append(v_new)
            y, k_new, v_new = odd_mixer_sample(rmsnorm(hs, g_mix), cache_diff_k, cache_diff_v, page_table, o,
                                               w_qkv_odd[o], diff_lambda[o], diff_subln[o], w_out_odd[o], lam_init,
                                               rel_bias_table)
            hs = hs + y
            dk_s.append(k_new); dv_s.append(v_new)
        mk = (mem_prompt @ w_mem_k[layer]).reshape(bp, MEM_LEN, MEM_HEADS, MEM_HEAD_DIM)
        mv = (mem_prompt @ w_mem_v[layer]).reshape(bp, MEM_LEN, MEM_HEADS, MEM_HEAD_DIM)
        mk_p.append(mk); mv_p.append(mv)
        hp = hp + memory_cross_attention(rmsnorm(hp, g_mem), mk, mv, w_mem_q[layer], w_mem_o[layer])
        hs = hs + memory_cross_attention(rmsnorm(hs, g_mem), cache_mem_k[layer], cache_mem_v[layer], w_mem_q[layer], w_mem_o[layer])
        if layer % 2 == 0:
            e = layer // 2
            hp = hp + swiglu(rmsnorm(hp, g_ffn), w_ffn_gu[e], w_ffn_down[e])
            hs = hs + swiglu(rmsnorm(hs, g_ffn), w_ffn_gu[e], w_ffn_down[e])
        else:
            o = layer // 2
            hp = hp + moe_swiglu(rmsnorm(hp, g_ffn), w_router[o], w_exp_gu[o], w_exp_down[o])
            hs = hs + moe_swiglu(rmsnorm(hs, g_ffn), w_router[o], w_exp_gu[o], w_exp_down[o])
    y_prompt = rmsnorm(hp, final_gain)
    y_sample = rmsnorm(hs, final_gain)
    return (y_prompt, y_sample,
            jnp.stack(pool_p), jnp.stack(pool_s),
            jnp.stack(swk_p), jnp.stack(swv_p), jnp.stack(swk_s), jnp.stack(swv_s),
            jnp.stack(dk_p, axis=2), jnp.stack(dv_p, axis=2), jnp.stack(dk_s, axis=2), jnp.stack(dv_s, axis=2),
            jnp.stack(mk_p), jnp.stack(mv_p))
```

```python
import functools
import math

import numpy as np
import jax
import jax.numpy as jnp
from jax import lax
from jax.experimental import pallas as pl
from jax.experimental.pallas import tpu as pltpu

F32 = jnp.float32
BF16 = jnp.bfloat16
EPS = 1e-6

REL_BUCKETS = 32
REL_MAX_EXACT = 16
REL_MAX_DIST = 128
REL_HEADS = 16
POOL_WINDOWS = (2, 4, 8, 16)
POOL_WIDTH = 1024
POOL_GROUP = POOL_WIDTH // len(POOL_WINDOWS)
POOL_STATE = max(POOL_WINDOWS) - 1
WINDOW = 128
SWA_HEADS = 16
SWA_KV_HEADS = 4
SWA_HEAD_DIM = 64
SWA_GROUP = SWA_HEADS // SWA_KV_HEADS
SWA_Q_W = SWA_HEADS * SWA_HEAD_DIM
SWA_KV_W = SWA_KV_HEADS * SWA_HEAD_DIM
DIFF_KV_HEADS = 4
DIFF_GROUP = 2
DIFF_HEAD_DIM = 128
DIFF_V_DIM = 256
DIFF_MAPS = DIFF_KV_HEADS * DIFF_GROUP * 2
DIFF_Q_W = DIFF_MAPS * DIFF_HEAD_DIM
DIFF_K_W = DIFF_KV_HEADS * 2 * DIFF_HEAD_DIM
DIFF_V_W = DIFF_KV_HEADS * DIFF_V_DIM
DIFF_OUT_W = DIFF_KV_HEADS * DIFF_GROUP * DIFF_V_DIM
MEM_HEADS = 4
MEM_HEAD_DIM = 128
MEM_W = MEM_HEADS * MEM_HEAD_DIM
N_EXPERTS = 8
TOP_K = 2

LANES = 128
VMEM_LIMIT = 56 * 1024 * 1024


def _params(*sem):
    return pltpu.CompilerParams(dimension_semantics=sem, vmem_limit_bytes=VMEM_LIMIT)


def _tile(n, pref):
    if n <= pref:
        return n
    t = pref
    while n % t:
        t -= 8
    return t


def _dot(a, b):
    return jnp.dot(a, b, preferred_element_type=F32)


def _dot_nt(a, b):
    return lax.dot_general(a, b, (((1,), (1,)), ((), ())), preferred_element_type=F32)


def _rmsnorm_kernel(x_ref, g_ref, o_ref):
    x = x_ref[...]
    y = x * lax.rsqrt(jnp.mean(x * x, axis=-1, keepdims=True) + EPS)
    o_ref[...] = (y * g_ref[...]).astype(o_ref.dtype)


def _rmsnorm(x, gain, out_dtype=BF16):
    m, d = x.shape
    tr = _tile(m, 512)
    return pl.pallas_call(
        _rmsnorm_kernel,
        out_shape=jax.ShapeDtypeStruct((m, d), out_dtype),
        grid=(m // tr,),
        in_specs=[pl.BlockSpec((tr, d), lambda i: (i, 0)),
                  pl.BlockSpec((1, d), lambda i: (0, 0))],
        out_specs=pl.BlockSpec((tr, d), lambda i: (i, 0)),
        compiler_params=_params("parallel"),
        name="rmsnorm",
    )(x, gain.reshape(1, d))


def _mm_kernel(*refs, has_resid):
    if has_resid:
        a_ref, w_ref, r_ref, o_ref = refs
    else:
        a_ref, w_ref, o_ref = refs
    acc = _dot(a_ref[...].astype(BF16), w_ref[...].astype(BF16))
    if has_resid:
        acc = r_ref[...] + acc
    o_ref[...] = acc.astype(o_ref.dtype)


def _matmul(a, w, resid=None, tm_pref=1024, tn_pref=512):
    m, k = a.shape
    n = w.shape[1]
    tm = _tile(m, tm_pref)
    tn = _tile(n, tn_pref)
    in_specs = [pl.BlockSpec((tm, k), lambda i, j: (i, 0)),
                pl.BlockSpec((k, tn), lambda i, j: (0, j))]
    args = [a, w]
    if resid is not None:
        in_specs.append(pl.BlockSpec((tm, tn), lambda i, j: (i, j)))
        args.append(resid)
    return pl.pallas_call(
        functools.partial(_mm_kernel, has_resid=resid is not None),
        out_shape=jax.ShapeDtypeStruct((m, n), F32),
        grid=(m // tm, n // tn),
        in_specs=in_specs,
        out_specs=pl.BlockSpec((tm, tn), lambda i, j: (i, j)),
        compiler_params=_params("parallel", "parallel"),
        name="matmul",
    )(*args)


def _swiglu_step(x, wg_ref, wu_ref, wd_ref):
    g = _dot(x, wg_ref[...].astype(BF16))
    u = _dot(x, wu_ref[...].astype(BF16))
    h = (g * jax.nn.sigmoid(g) * u).astype(BF16)
    return _dot(h, wd_ref[...].astype(BF16))


def _ffn_kernel(x_ref, wg_ref, wu_ref, wd_ref, r_ref, o_ref):
    @pl.when(pl.program_id(1) == 0)
    def _():
        o_ref[...] = r_ref[...]

    o_ref[...] += _swiglu_step(x_ref[...], wg_ref, wu_ref, wd_ref)


def _ffn(x, w_gu, w_down, resid, tm_pref=1024, tf_pref=256):
    m, d = x.shape
    f = w_down.shape[0]
    tm = _tile(m, tm_pref)
    tf = _tile(f, tf_pref)
    nf = f // tf
    return pl.pallas_call(
        _ffn_kernel,
        out_shape=jax.ShapeDtypeStruct((m, d), F32),
        grid=(m // tm, nf),
        in_specs=[pl.BlockSpec((tm, d), lambda i, j: (i, 0)),
                  pl.BlockSpec((d, tf), lambda i, j: (0, j)),
                  pl.BlockSpec((d, tf), lambda i, j: (0, j + nf)),
                  pl.BlockSpec((tf, d), lambda i, j: (j, 0)),
                  pl.BlockSpec((tm, d), lambda i, j: (i, 0))],
        out_specs=pl.BlockSpec((tm, d), lambda i, j: (i, 0)),
        compiler_params=_params("parallel", "arbitrary"),
        name="ffn_swiglu",
    )(x, w_gu, w_gu, w_down, resid)


def _moe_kernel(te_ref, tv_ref, x_ref, wg_ref, wu_ref, wd_ref, o_ref):
    i = pl.program_id(0)

    @pl.when(tv_ref[i] > 0)
    def _():
        y = _swiglu_step(x_ref[...].astype(BF16), wg_ref, wu_ref, wd_ref)

        @pl.when(pl.program_id(1) == 0)
        def _():
            o_ref[...] = y

        @pl.when(pl.program_id(1) > 0)
        def _():
            o_ref[...] += y


def _moe_experts(x_sorted, tile_expert, tile_valid, w_gu, w_down, tm, tf_pref=256):
    np_rows, d = x_sorted.shape
    f = w_down.shape[1]
    tf = _tile(f, tf_pref)
    nf = f // tf

    def chunk(j, tv, i):
        return jnp.where(tv[i] > 0, j, nf - 1)

    grid_spec = pltpu.PrefetchScalarGridSpec(
        num_scalar_prefetch=2,
        grid=(np_rows // tm, nf),
        in_specs=[pl.BlockSpec((tm, d), lambda i, j, te, tv: (i, 0)),
                  pl.BlockSpec((None, d, tf), lambda i, j, te, tv: (te[i], 0, chunk(j, tv, i))),
                  pl.BlockSpec((None, d, tf), lambda i, j, te, tv: (te[i], 0, chunk(j, tv, i) + nf)),
                  pl.BlockSpec((None, tf, d), lambda i, j, te, tv: (te[i], chunk(j, tv, i), 0))],
        out_specs=pl.BlockSpec((tm, d), lambda i, j, te, tv: (i, 0)),
    )
    return pl.pallas_call(
        _moe_kernel,
        out_shape=jax.ShapeDtypeStruct((np_rows, d), F32),
        grid_spec=grid_spec,
        compiler_params=_params("arbitrary", "arbitrary"),
        name="moe_experts",
    )(tile_expert, tile_valid, x_sorted, w_gu, w_gu, w_down)


def _router_kernel(x_ref, w_ref, idx_ref, gate_ref):
    lg = _dot(x_ref[...].astype(BF16), w_ref[...].astype(BF16))
    col = lax.broadcasted_iota(jnp.int32, lg.shape, 1)
    lg = jnp.where(col < N_EXPERTS, lg, -jnp.inf)
    v1 = jnp.max(lg, axis=-1, keepdims=True)
    i1 = jnp.min(jnp.where(lg == v1, col, LANES), axis=-1, keepdims=True)
    lg2 = jnp.where(col == i1, -jnp.inf, lg)
    v2 = jnp.max(lg2, axis=-1, keepdims=True)
    i2 = jnp.min(jnp.where(lg2 == v2, col, LANES), axis=-1, keepdims=True)
    e2 = jnp.exp(v2 - v1)
    den = 1.0 + e2
    idx_ref[...] = jnp.where(col == 0, i1, jnp.where(col == 1, i2, 0))
    gate_ref[...] = jnp.where(col == 0, 1.0 / den, jnp.where(col == 1, e2 / den, 0.0))


def _router(x, w_router):
    m, d = x.shape
    w_pad = jnp.zeros((d, LANES), F32).at[:, :N_EXPERTS].set(w_router)
    tm = _tile(m, 640)
    return pl.pallas_call(
        _router_kernel,
        out_shape=(jax.ShapeDtypeStruct((m, LANES), jnp.int32),
                   jax.ShapeDtypeStruct((m, LANES), F32)),
        grid=(m // tm,),
        in_specs=[pl.BlockSpec((tm, d), lambda i: (i, 0)),
                  pl.BlockSpec((d, LANES), lambda i: (0, 0))],
        out_specs=(pl.BlockSpec((tm, LANES), lambda i: (i, 0)),
                   pl.BlockSpec((tm, LANES), lambda i: (i, 0))),
        compiler_params=_params("parallel"),
        name="router_top2",
    )(x, w_pad)


def _row_copy(src_hbm, row, dst_ref, r, sem):
    return pltpu.make_async_copy(src_hbm.at[pl.ds(row, 1)], dst_ref.at[pl.ds(r, 1)], sem)


def _gather_kernel(src_ref, x_hbm, o_ref, sem, *, tg):
    base = pl.program_id(0) * tg

    def issue(r, c):
        _row_copy(x_hbm, src_ref[base + r], o_ref, r, sem).start()
        return c

    lax.fori_loop(0, tg, issue, 0)

    def drain(r, c):
        _row_copy(x_hbm, 0, o_ref, r, sem).wait()
        return c

    lax.fori_loop(0, tg, drain, 0)


def _gather_rows(x, src, tg=256):
    n = src.shape[0]
    d = x.shape[1]
    grid_spec = pltpu.PrefetchScalarGridSpec(
        num_scalar_prefetch=1,
        grid=(n // tg,),
        in_specs=[pl.BlockSpec(memory_space=pl.ANY)],
        out_specs=pl.BlockSpec((tg, d), lambda i, s: (i, 0)),
        scratch_shapes=[pltpu.SemaphoreType.DMA(())],
    )
    return pl.pallas_call(
        functools.partial(_gather_kernel, tg=tg),
        out_shape=jax.ShapeDtypeStruct((n, d), x.dtype),
        grid_spec=grid_spec,
        compiler_params=_params("arbitrary"),
        name="gather_rows",
    )(src, x)


def _combine_kernel(p1_ref, p2_ref, y_hbm, gate_ref, r_ref, fg_ref, o_ref, buf, sem, *, tc):
    base = pl.program_id(0) * tc

    def issue(r, c):
        _row_copy(y_hbm, p1_ref[base + r], buf.at[0], r, sem).start()
        _row_copy(y_hbm, p2_ref[base + r], buf.at[1], r, sem).start()
        return c

    lax.fori_loop(0, tc, issue, 0)

    def drain(r, c):
        _row_copy(y_hbm, 0, buf.at[0], r, sem).wait()
        _row_copy(y_hbm, 0, buf.at[1], r, sem).wait()
        return c

    lax.fori_loop(0, tc, drain, 0)

    g = gate_ref[...]
    h = r_ref[...] + (buf[0] * g[:, 0:1] + buf[1] * g[:, 1:2])
    y = h * lax.rsqrt(jnp.mean(h * h, axis=-1, keepdims=True) + EPS)
    o_ref[...] = y * fg_ref[...]


def _moe_combine_norm(y_sorted, pos1, pos2, gates, resid, final_gain, tc=128):
    m, d = resid.shape
    grid_spec = pltpu.PrefetchScalarGridSpec(
        num_scalar_prefetch=2,
        grid=(m // tc,),
        in_specs=[pl.BlockSpec(memory_space=pl.ANY),
                  pl.BlockSpec((tc, LANES), lambda i, a, b: (i, 0)),
                  pl.BlockSpec((tc, d), lambda i, a, b: (i, 0)),
                  pl.BlockSpec((1, d), lambda i, a, b: (0, 0))],
        out_specs=pl.BlockSpec((tc, d), lambda i, a, b: (i, 0)),
        scratch_shapes=[pltpu.VMEM((2, tc, d), F32), pltpu.SemaphoreType.DMA(())],
    )
    return pl.pallas_call(
        functools.partial(_combine_kernel, tc=tc),
        out_shape=jax.ShapeDtypeStruct((m, d), F32),
        grid_spec=grid_spec,
        compiler_params=_params("arbitrary"),
        name="moe_combine_norm",
    )(pos1, pos2, y_sorted, gates, resid, final_gain.reshape(1, d))


def _bucket_thresholds():
    n = np.arange(REL_MAX_EXACT, 4 * REL_MAX_DIST, dtype=np.int64)
    ratio = np.log(n.astype(np.float32) / np.float32(REL_MAX_EXACT))
    large = REL_MAX_EXACT + (ratio / np.float32(math.log(REL_MAX_DIST / REL_MAX_EXACT))
                             * np.float32(REL_BUCKETS - REL_MAX_EXACT)).astype(np.int32)
    large = np.minimum(large, REL_BUCKETS - 1)
    return [int(n[np.argmax(large >= b)]) for b in range(REL_MAX_EXACT + 1, REL_BUCKETS)]


def _bias_kernel(t_ref, o_ref, *, thresholds):
    nd = o_ref.shape[1]
    dist = lax.broadcasted_iota(jnp.int32, (1, nd), 1)
    bucket = jnp.minimum(dist, REL_MAX_EXACT)
    for t in thresholds:
        bucket = bucket + (dist >= t).astype(jnp.int32)
    tt = t_ref[...]
    acc = jnp.zeros(o_ref.shape, F32)
    for b in range(REL_BUCKETS):
        acc = jnp.where(bucket == b, tt[:, b:b + 1], acc)
    o_ref[...] = acc


def _bias_by_distance(rel_table, nd=2 * LANES):
    return pl.pallas_call(
        functools.partial(_bias_kernel, thresholds=_bucket_thresholds()),
        out_shape=jax.ShapeDtypeStruct((REL_HEADS, nd), F32),
        name="rel_bias_by_distance",
    )(rel_table.T)


def _bias_tile(bias_d, dist):
    nd = bias_d.shape[1]
    return bias_d[:, np.clip(dist, 0, nd - 1)]


def _pool_project(pooled, map_ref, scale_ref, o_ref):
    for g in range(len(POOL_WINDOWS)):
        cols = slice(g * POOL_GROUP, (g + 1) * POOL_GROUP)
        mixed = _dot(pooled[g].astype(BF16), map_ref[g].astype(BF16))
        o_ref[:, cols] = (mixed * scale_ref[:, cols]).astype(o_ref.dtype)


def _pool_prompt_kernel(u_ref, halo_ref, map_ref, scale_ref, o_ref, *, tp):
    i = pl.program_id(0)
    halo_rows = halo_ref.shape[0]
    halo = jnp.where(i > 0, halo_ref[...], 0.0)
    ext = jnp.concatenate([halo, u_ref[...]], axis=0)
    row = i * tp + lax.broadcasted_iota(jnp.int32, (tp, 1), 0)
    pooled = []
    for g, w in enumerate(POOL_WINDOWS):
        cols = slice(g * POOL_GROUP, (g + 1) * POOL_GROUP)
        s = ext[:, cols]
        shift = 1
        while shift < w:
            s = s + pltpu.roll(s, shift, axis=0)
            shift *= 2
        cnt = jnp.minimum(row + 1, w).astype(F32)
        x = ext[halo_rows:, cols]
        pooled.append(s[halo_rows:] / cnt - x)
    _pool_project(pooled, map_ref, scale_ref, o_ref)


def _pool_prompt(u, pool_map, pool_scale, tp=256):
    s = u.shape[0]
    tp = _tile(s, tp)
    halo = 16
    return pl.pallas_call(
        functools.partial(_pool_prompt_kernel, tp=tp),
        out_shape=jax.ShapeDtypeStruct((s, POOL_WIDTH), BF16),
        grid=(s // tp,),
        in_specs=[pl.BlockSpec((tp, POOL_WIDTH), lambda i: (i, 0)),
                  pl.BlockSpec((halo, POOL_WIDTH), lambda i: (jnp.maximum(i * (tp // halo) - 1, 0), 0)),
                  pl.BlockSpec(pool_map.shape, lambda i: (0, 0, 0)),
                  pl.BlockSpec((1, POOL_WIDTH), lambda i: (0, 0))],
        out_specs=pl.BlockSpec((tp, POOL_WIDTH), lambda i: (i, 0)),
        compiler_params=_params("parallel"),
        name="pool_prompt",
    )(u, u, pool_map, pool_scale.reshape(1, POOL_WIDTH))


def _pool_sample_kernel(st_ref, u_ref, map_ref, scale_ref, o_ref):
    pooled = []
    for g, w in enumerate(POOL_WINDOWS):
        cols = slice(g * POOL_GROUP, (g + 1) * POOL_GROUP)
        x = u_ref[:, cols]
        s = x
        for k in range(1, w):
            s = s + st_ref[:, POOL_STATE - k, cols]
        pooled.append(s / float(w) - x)
    _pool_project(pooled, map_ref, scale_ref, o_ref)


def _pool_sample(state, u, pool_map, pool_scale):
    db = state.shape[0]
    return pl.pallas_call(
        _pool_sample_kernel,
        out_shape=jax.ShapeDtypeStruct((db, POOL_WIDTH), BF16),
        grid=(1,),
        in_specs=[pl.BlockSpec(state.shape, lambda i: (0, 0, 0)),
                  pl.BlockSpec((db, POOL_WIDTH), lambda i: (0, 0)),
                  pl.BlockSpec(pool_map.shape, lambda i: (0, 0, 0)),
                  pl.BlockSpec((1, POOL_WIDTH), lambda i: (0, 0))],
        out_specs=pl.BlockSpec((db, POOL_WIDTH), lambda i: (0, 0)),
        compiler_params=_params("arbitrary"),
        name="pool_sample",
    )(state, u, pool_map, pool_scale.reshape(1, POOL_WIDTH))


def _swa_prompt_kernel(q_ref, kp_ref, kc_ref, vp_ref, vc_ref, bias_ref, sink_ref, o_ref):
    i = pl.program_id(0)
    col = lax.broadcasted_iota(jnp.int32, (WINDOW, 2 * WINDOW), 1)
    key_ok = (i > 0) | (col >= WINDOW)
    k_all = jnp.concatenate([kp_ref[...], kc_ref[...]], axis=0).astype(BF16)
    v_all = jnp.concatenate([vp_ref[...], vc_ref[...]], axis=0).astype(BF16)
    scale = SWA_HEAD_DIM ** -0.5
    for h in range(SWA_KV_HEADS):
        kv_cols = slice(h * SWA_HEAD_DIM, (h + 1) * SWA_HEAD_DIM)
        k_h = k_all[:, kv_cols]
        v_h = v_all[:, kv_cols]
        for g in range(SWA_GROUP):
            head = h * SWA_GROUP + g
            q_cols = slice(head * SWA_HEAD_DIM, (head + 1) * SWA_HEAD_DIM)
            s = _dot_nt(q_ref[:, q_cols].astype(BF16), k_h) * scale + bias_ref[head]
            s = jnp.where(key_ok, s, -jnp.inf)
            sink = sink_ref[head]
            m = jnp.maximum(jnp.max(s, axis=-1, keepdims=True), sink)
            p = jnp.exp(s - m)
            p = p / (jnp.sum(p, axis=-1, keepdims=True) + jnp.exp(sink - m))
            o_ref[:, q_cols] = _dot(p.astype(BF16), v_h).astype(o_ref.dtype)


def _swa_prompt(u, bias_d, sinks):
    s = u.shape[0]
    nb = s // WINDOW
    qb = POOL_WIDTH // SWA_Q_W
    kb = (POOL_WIDTH + SWA_Q_W) // SWA_KV_W
    i_idx = np.arange(WINDOW)[:, None]
    j_idx = np.arange(2 * WINDOW)[None, :]
    dist = WINDOW + i_idx - j_idx
    bias = jnp.where((dist >= 0) & (dist <= WINDOW), _bias_tile(bias_d, dist), -jnp.inf)
    prev = lambda i: (jnp.maximum(i - 1, 0), kb)
    return pl.pallas_call(
        _swa_prompt_kernel,
        out_shape=jax.ShapeDtypeStruct((s, SWA_Q_W), BF16),
        grid=(nb,),
        in_specs=[pl.BlockSpec((WINDOW, SWA_Q_W), lambda i: (i, qb)),
                  pl.BlockSpec((WINDOW, SWA_KV_W), prev),
                  pl.BlockSpec((WINDOW, SWA_KV_W), lambda i: (i, kb)),
                  pl.BlockSpec((WINDOW, SWA_KV_W), lambda i: (jnp.maximum(i - 1, 0), kb + 1)),
                  pl.BlockSpec((WINDOW, SWA_KV_W), lambda i: (i, kb + 1)),
                  pl.BlockSpec(bias.shape, lambda i: (0, 0, 0)),
                  pl.BlockSpec(memory_space=pltpu.SMEM)],
        out_specs=pl.BlockSpec((WINDOW, SWA_Q_W), lambda i: (i, 0)),
        compiler_params=_params("parallel"),
        name="swa_prompt",
    )(u, u, u, u, u, bias, sinks)


def _swa_sample_kernel(q_ref, k_ref, v_ref, k0_ref, v0_ref, bias_ref, bias0_ref, sink_ref, o_ref):
    scale = SWA_HEAD_DIM ** -0.5
    lane = lax.broadcasted_iota(jnp.int32, (1, SWA_HEADS, SWA_KV_W), 2)
    row = lax.broadcasted_iota(jnp.int32, (1, SWA_HEADS, SWA_KV_W), 1)
    own = (lane // SWA_HEAD_DIM) == (row // SWA_GROUP)
    q_bd = jnp.where(own, q_ref[...], 0.0).astype(BF16)
    k = k_ref[...].astype(BF16)
    v = v_ref[...].astype(BF16)
    s = jnp.einsum('bhc,bkc->bhk', q_bd, k, preferred_element_type=F32) * scale + bias_ref[...][None]
    k0 = k0_ref[...].astype(BF16).astype(F32)
    s0 = jnp.sum(q_bd.astype(F32) * k0, axis=-1, keepdims=True) * scale + bias0_ref[...][None]
    sink = sink_ref[...][None]
    m = jnp.maximum(jnp.maximum(jnp.max(s, axis=-1, keepdims=True), s0), sink)
    p = jnp.exp(s - m)
    p0 = jnp.exp(s0 - m)
    den = jnp.sum(p, axis=-1, keepdims=True) + p0 + jnp.exp(sink - m)
    o = jnp.einsum('bhk,bkc->bhc', (p / den).astype(BF16), v, preferred_element_type=F32)
    o = o + (p0 / den).astype(BF16).astype(F32) * v0_ref[...].astype(BF16).astype(F32)
    for h in range(SWA_KV_HEADS):
        rows = slice(h * SWA_GROUP, (h + 1) * SWA_GROUP)
        o_ref[:, rows, :] = o[:, rows, h * SWA_HEAD_DIM:(h + 1) * SWA_HEAD_DIM].astype(o_ref.dtype)


def _swa_sample(q, k_state, v_state, k0, v0, bias_d, sinks, sb=8):
    db = q.shape[0]
    sb = _tile(db, sb)
    q_t = jnp.tile(q, (1, 1, SWA_KV_HEADS))
    bias = _bias_tile(bias_d, WINDOW - 1 - np.arange(WINDOW))
    bias0 = bias_d[:, WINDOW:WINDOW + 1]
    blk3 = lambda a, b: pl.BlockSpec((sb, a, b), lambda i: (i, 0, 0))
    o = pl.pallas_call(
        _swa_sample_kernel,
        out_shape=jax.ShapeDtypeStruct((db, SWA_HEADS, SWA_HEAD_DIM), BF16),
        grid=(db // sb,),
        in_specs=[blk3(SWA_HEADS, SWA_KV_W), blk3(WINDOW, SWA_KV_W), blk3(WINDOW, SWA_KV_W),
                  blk3(1, SWA_KV_W), blk3(1, SWA_KV_W),
                  pl.BlockSpec((SWA_HEADS, WINDOW), lambda i: (0, 0)),
                  pl.BlockSpec((SWA_HEADS, 1), lambda i: (0, 0)),
                  pl.BlockSpec((SWA_HEADS, 1), lambda i: (0, 0))],
        out_specs=blk3(SWA_HEADS, SWA_HEAD_DIM),
        compiler_params=_params("parallel"),
        name="swa_sample",
    )(q_t, k_state, v_state, k0, v0, bias, bias0, sinks.reshape(SWA_HEADS, 1))
    return o.reshape(db, SWA_Q_W)


def _mem_prompt_kernel(q_ref, k_ref, v_ref, o_ref):
    scale = MEM_HEAD_DIM ** -0.5
    for h in range(MEM_HEADS):
        cols = slice(h * MEM_HEAD_DIM, (h + 1) * MEM_HEAD_DIM)
        s = _dot_nt(q_ref[:, cols].astype(BF16), k_ref[:, cols].astype(BF16)) * scale
        p = jnp.exp(s - jnp.max(s, axis=-1, keepdims=True))
        p = p / jnp.sum(p, axis=-1, keepdims=True)
        o_ref[:, cols] = _dot(p.astype(BF16), v_ref[:, cols].astype(BF16)).astype(o_ref.dtype)


def _mem_prompt(q, mk, mv, tq=512):
    s = q.shape[0]
    tq = _tile(s, tq)
    return pl.pallas_call(
        _mem_prompt_kernel,
        out_shape=jax.ShapeDtypeStruct((s, MEM_W), BF16),
        grid=(s // tq,),
        in_specs=[pl.BlockSpec((tq, MEM_W), lambda i: (i, 0)),
                  pl.BlockSpec(mk.shape, lambda i: (0, 0)),
                  pl.BlockSpec(mv.shape, lambda i: (0, 0))],
        out_specs=pl.BlockSpec((tq, MEM_W), lambda i: (i, 0)),
        compiler_params=_params("parallel"),
        name="mem_attn_prompt",
    )(q, mk, mv)


def _mem_sample_kernel(q_ref, k_ref, v_ref, o_ref):
    scale = MEM_HEAD_DIM ** -0.5
    lane = lax.broadcasted_iota(jnp.int32, (1, MEM_HEADS, MEM_W), 2)
    row = lax.broadcasted_iota(jnp.int32, (1, MEM_HEADS, MEM_W), 1)
    own = (lane // MEM_HEAD_DIM) == row
    q_bd = jnp.where(own, q_ref[...], 0.0).astype(BF16)
    s = jnp.einsum('bhc,bkc->bhk', q_bd, k_ref[...].astype(BF16), preferred_element_type=F32) * scale
    p = jnp.exp(s - jnp.max(s, axis=-1, keepdims=True))
    p = p / jnp.sum(p, axis=-1, keepdims=True)
    o = jnp.einsum('bhk,bkc->bhc', p.astype(BF16), v_ref[...].astype(BF16), preferred_element_type=F32)
    o_ref[...] = jnp.sum(jnp.where(own, o, 0.0), axis=1, keepdims=True).astype(o_ref.dtype)


def _mem_sample(q, mem_k, mem_v, sb=8):
    db, ml, _ = mem_k.shape
    sb = _tile(db, sb)
    o = pl.pallas_call(
        _mem_sample_kernel,
        out_shape=jax.ShapeDtypeStruct((db, 1, MEM_W), BF16),
        grid=(db // sb,),
        in_specs=[pl.BlockSpec((sb, 1, MEM_W), lambda i: (i, 0, 0)),
                  pl.BlockSpec((sb, ml, MEM_W), lambda i: (i, 0, 0)),
                  pl.BlockSpec((sb, ml, MEM_W), lambda i: (i, 0, 0))],
        out_specs=pl.BlockSpec((sb, 1, MEM_W), lambda i: (i, 0, 0)),
        compiler_params=_params("parallel"),
        name="mem_attn_sample",
    )(q.reshape(db, 1, MEM_W), mem_k, mem_v)
    return o.reshape(db, MEM_W)


def _lambda_value(lam_ref, lam_init):
    lp = lam_ref[...]
    a = jnp.sum(lp[0:1] * lp[1:2], axis=-1, keepdims=True)
    b = jnp.sum(lp[2:3] * lp[3:4], axis=-1, keepdims=True)
    return jnp.exp(a) - jnp.exp(b) + lam_init


def _diff_finish(o0, o1, lam, subln, lam_init):
    o = o0 - lam * o1
    o = o * lax.rsqrt(jnp.mean(o * o, axis=-1, keepdims=True) + EPS)
    return o * subln * (1.0 - lam_init)


def _diff_prompt_kernel(qi_ref, kj_ref, q_ref, k_ref, v_ref, bias_ref, far_ref, lam_ref, subln_ref,
                        o_ref, m_sc, l_sc, acc_sc, *, lam_init):
    h = pl.program_id(0)
    step = pl.program_id(1)
    qi = qi_ref[step]
    kj = kj_ref[step]
    scale = DIFF_HEAD_DIM ** -0.5
    combos = [(g, mp) for g in range(DIFF_GROUP) for mp in range(2)]

    @pl.when(kj == 0)
    def _():
        m_sc[...] = jnp.full(m_sc.shape, -jnp.inf, F32)
        l_sc[...] = jnp.zeros(l_sc.shape, F32)
        acc_sc[...] = jnp.zeros(acc_sc.shape, F32)

    def update(bias_of):
        v = v_ref[...].astype(BF16)
        for c, (g, mp) in enumerate(combos):
            q = q_ref[:, pl.ds((g * 2 + mp) * DIFF_HEAD_DIM, DIFF_HEAD_DIM)].astype(BF16)
            k = k_ref[:, pl.ds(mp * DIFF_HEAD_DIM, DIFF_HEAD_DIM)].astype(BF16)
            s = _dot_nt(q, k) * scale + bias_of(c)
            m_old = m_sc[c]
            m_new = jnp.maximum(m_old, jnp.max(s, axis=-1, keepdims=True))
            p = jnp.exp(s - m_new)
            alpha = jnp.exp(m_old - m_new)
            l_sc[c] = alpha * l_sc[c] + jnp.sum(p, axis=-1, keepdims=True)
            acc_sc[c] = alpha * acc_sc[c] + _dot(p.astype(BF16), v)
            m_sc[c] = m_new

    @pl.when(kj < qi - 1)
    def _():
        update(lambda c: far_ref[h * 4 + c])

    @pl.when(kj == qi - 1)
    def _():
        update(lambda c: bias_ref[1, c])

    @pl.when(kj == qi)
    def _():
        update(lambda c: bias_ref[0, c])
        lam = _lambda_value(lam_ref, lam_init)
        for g in range(DIFF_GROUP):
            o0 = acc_sc[2 * g] / l_sc[2 * g]
            o1 = acc_sc[2 * g + 1] / l_sc[2 * g + 1]
            o_ref[:, pl.ds(g * DIFF_V_DIM, DIFF_V_DIM)] = _diff_finish(
                o0, o1, lam, subln_ref[...], lam_init).astype(o_ref.dtype)


def _diff_prompt(u, bias_d, rel_table, lam_params, subln, lam_init, t_pref=256):
    s = u.shape[0]
    t = _tile(s, t_pref)
    nq = s // t
    pairs = [(qi, kj) for qi in range(nq) for kj in range(qi + 1)]
    qi_arr = jnp.asarray([p[0] for p in pairs], jnp.int32)
    kj_arr = jnp.asarray([p[1] for p in pairs], jnp.int32)
    i_idx = np.arange(t)[:, None]
    j_idx = np.arange(t)[None, :]
    d0 = i_idx - j_idx
    diag = jnp.where(d0 >= 0, _bias_tile(bias_d, d0), -jnp.inf)
    sub = _bias_tile(bias_d, d0 + t)
    bias = jnp.stack([diag, sub]).reshape(2, DIFF_KV_HEADS, 4, t, t)
    far = rel_table[REL_BUCKETS - 1]
    kb = DIFF_Q_W // (2 * DIFF_HEAD_DIM)
    vb = (DIFF_Q_W + DIFF_K_W) // DIFF_V_DIM
    grid_spec = pltpu.PrefetchScalarGridSpec(
        num_scalar_prefetch=2,
        grid=(DIFF_KV_HEADS, len(pairs)),
        in_specs=[pl.BlockSpec((t, 4 * DIFF_HEAD_DIM), lambda h, st, qi, kj: (qi[st], h)),
                  pl.BlockSpec((t, 2 * DIFF_HEAD_DIM), lambda h, st, qi, kj: (kj[st], kb + h)),
                  pl.BlockSpec((t, DIFF_V_DIM), lambda h, st, qi, kj: (kj[st], vb + h)),
                  pl.BlockSpec((2, None, 4, t, t), lambda h, st, qi, kj: (0, h, 0, 0, 0)),
                  pl.BlockSpec(memory_space=pltpu.SMEM),
                  pl.BlockSpec((4, DIFF_HEAD_DIM), lambda h, st, qi, kj: (0, 0)),
                  pl.BlockSpec((1, DIFF_V_DIM), lambda h, st, qi, kj: (0, 0))],
        out_specs=pl.BlockSpec((t, DIFF_GROUP * DIFF_V_DIM), lambda h, st, qi, kj: (qi[st], h)),
        scratch_shapes=[pltpu.VMEM((4, t, 1), F32), pltpu.VMEM((4, t, 1), F32),
                        pltpu.VMEM((4, t, DIFF_V_DIM), F32)],
    )
    assert t >= REL_MAX_DIST, "far blocks must lie entirely in the last distance bucket"
    return pl.pallas_call(
        functools.partial(_diff_prompt_kernel, lam_init=lam_init),
        out_shape=jax.ShapeDtypeStruct((s, DIFF_OUT_W), BF16),
        grid_spec=grid_spec,
        compiler_params=_params("arbitrary", "arbitrary"),
        name="diff_attn_prompt",
    )(qi_arr, kj_arr, u, u, u, bias, far, lam_params, subln.reshape(1, DIFF_V_DIM))


def _diff_sample_kernel(pt_ref, q_ref, kn_ref, vn_ref, bias_ref, bias0_ref, lam_ref, subln_ref, *rest,
                        pages, lam_init):
    k_refs = rest[:pages]
    v_refs = rest[pages:2 * pages]
    o_ref, m_sc, l_sc, acc_sc = rest[2 * pages:]
    c = pl.program_id(1)
    scale = DIFF_HEAD_DIM ** -0.5
    page = k_refs[0].shape[0] // 8
    kw = DIFF_K_W
    lane = lax.broadcasted_iota(jnp.int32, (DIFF_MAPS, kw), 1)
    row = lax.broadcasted_iota(jnp.int32, (DIFF_MAPS, kw), 0)
    q_bd = jnp.where((lane // DIFF_HEAD_DIM) == (row // DIFF_GROUP), q_ref[...], 0.0).astype(BF16)

    @pl.when(c == 0)
    def _():
        kn = kn_ref[...].astype(BF16).astype(F32)
        s0 = jnp.sum(q_bd.astype(F32) * kn, axis=-1, keepdims=True) * scale + bias0_ref[...]
        m_sc[...] = s0
        l_sc[...] = jnp.ones(l_sc.shape, F32)
        acc_sc[...] = jnp.broadcast_to(vn_ref[...].astype(BF16).astype(F32), acc_sc.shape)

    def matrix(ref):
        return jnp.concatenate([ref[pl.ds(r, page, stride=8), :] for r in range(8)], axis=1).astype(BF16)

    s = jnp.concatenate([_dot_nt(q_bd, matrix(k_refs[i])) for i in range(pages)], axis=1)
    s = s * scale + bias_ref[:, pl.ds(pl.multiple_of(c * (pages * page), LANES), pages * page)]
    m_old = m_sc[...]
    m_new = jnp.maximum(m_old, jnp.max(s, axis=-1, keepdims=True))
    p = jnp.exp(s - m_new)
    alpha = jnp.exp(m_old - m_new)
    l_sc[...] = alpha * l_sc[...] + jnp.sum(p, axis=-1, keepdims=True)
    pb = p.astype(BF16)
    pv = _dot(pb[:, 0:page], matrix(v_refs[0]))
    for i in range(1, pages):
        pv = pv + _dot(pb[:, i * page:(i + 1) * page], matrix(v_refs[i]))
    acc_sc[...] = alpha * acc_sc[...] + pv
    m_sc[...] = m_new

    @pl.when(c == pl.num_programs(1) - 1)
    def _():
        lam = _lambda_value(lam_ref, lam_init)
        o = acc_sc[...] / l_sc[...]
        for h in range(DIFF_KV_HEADS):
            for g in range(DIFF_GROUP):
                r0 = h * 4 + g
                r1 = h * 4 + 2 + g
                cols = slice(h * DIFF_V_DIM, (h + 1) * DIFF_V_DIM)
                res = _diff_finish(o[r0:r0 + 1, cols], o[r1:r1 + 1, cols], lam, subln_ref[...], lam_init)
                o_ref[:, pl.ds((h * DIFF_GROUP + g) * DIFF_V_DIM, DIFF_V_DIM)] = res.astype(o_ref.dtype)


def _diff_sample(u, cache_k, cache_v, page_table, bias_d, lam_params, subln, lam_init, pages=8):
    db = u.shape[0]
    n_pages = page_table.shape[1]
    page = cache_k.shape[1] // 8
    pages = min(pages, n_pages)
    assert n_pages % pages == 0
    past = n_pages * page
    order = np.array([h * 4 + g * 2 + mp for h in range(DIFF_KV_HEADS) for mp in range(2) for g in range(DIFF_GROUP)])
    q = u[:, :DIFF_Q_W].reshape(db, DIFF_KV_HEADS, DIFF_GROUP, 2, DIFF_HEAD_DIM)
    q = q.transpose(0, 1, 3, 2, 4).reshape(db, DIFF_MAPS, DIFF_HEAD_DIM)
    q_t = jnp.tile(q, (1, 1, DIFF_K_W // DIFF_HEAD_DIM))
    k_new = u[:, DIFF_Q_W:DIFF_Q_W + DIFF_K_W].reshape(db, 1, DIFF_K_W)
    v_new = u[:, DIFF_Q_W + DIFF_K_W:].reshape(db, 1, DIFF_V_W)
    bias = _bias_tile(bias_d, past - np.arange(past))[order]
    bias0 = bias_d[order, 0:1]
    pt_flat = page_table.reshape(-1)

    def page_spec(i):
        return pl.BlockSpec((None, page * 8, LANES),
                            lambda b, c, pt: (pt[b * n_pages + c * pages + i], 0, 0))

    const2 = lambda b, c, pt: (0, 0)
    grid_spec = pltpu.PrefetchScalarGridSpec(
        num_scalar_prefetch=1,
        grid=(db, n_pages // pages),
        in_specs=[pl.BlockSpec((None, DIFF_MAPS, DIFF_K_W), lambda b, c, pt: (b, 0, 0)),
                  pl.BlockSpec((None, 1, DIFF_K_W), lambda b, c, pt: (b, 0, 0)),
                  pl.BlockSpec((None, 1, DIFF_V_W), lambda b, c, pt: (b, 0, 0)),
                  pl.BlockSpec(bias.shape, const2),
                  pl.BlockSpec((DIFF_MAPS, 1), const2),
                  pl.BlockSpec((4, DIFF_HEAD_DIM), const2),
                  pl.BlockSpec((1, DIFF_V_DIM), const2)]
                 + [page_spec(i) for i in range(pages)] * 2,
        out_specs=pl.BlockSpec((None, 1, DIFF_OUT_W), lambda b, c, pt: (b, 0, 0)),
        scratch_shapes=[pltpu.VMEM((DIFF_MAPS, 1), F32), pltpu.VMEM((DIFF_MAPS, 1), F32),
                        pltpu.VMEM((DIFF_MAPS, DIFF_V_W), F32)],
    )
    o = pl.pallas_call(
        functools.partial(_diff_sample_kernel, pages=pages, lam_init=lam_init),
        out_shape=jax.ShapeDtypeStruct((db, 1, DIFF_OUT_W), BF16),
        grid_spec=grid_spec,
        compiler_params=_params("arbitrary", "arbitrary"),
        name="diff_attn_sample",
    )(pt_flat, q_t, k_new, v_new, bias, bias0, lam_params, subln.reshape(1, DIFF_V_DIM),
      *([cache_k] * pages), *([cache_v] * pages))
    return o.reshape(db, DIFF_OUT_W)


def _routing_tables(idx, tm):
    m = idx.shape[0]
    e_flat = idx.reshape(-1)
    onehot = (e_flat[:, None] == jnp.arange(N_EXPERTS)[None, :]).astype(jnp.int32)
    rank = jnp.sum((jnp.cumsum(onehot, axis=0) - onehot) * onehot, axis=1)
    counts = jnp.sum(onehot, axis=0)
    tiles = (counts + tm - 1) // tm
    tile_end = jnp.cumsum(tiles)
    tile_start = tile_end - tiles
    pos = tile_start[e_flat] * tm + rank
    n_tiles = (TOP_K * m + tm - 1) // tm + N_EXPERTS
    t_ids = jnp.arange(n_tiles)
    tile_valid = (t_ids < tile_end[-1]).astype(jnp.int32)
    last_used = jnp.max(jnp.where(counts > 0, jnp.arange(N_EXPERTS), 0))
    tile_expert = jnp.minimum(jnp.searchsorted(tile_end, t_ids, side='right'), last_used).astype(jnp.int32)
    src = jnp.zeros((n_tiles * tm,), jnp.int32).at[pos].set(jnp.arange(TOP_K * m, dtype=jnp.int32) // TOP_K)
    pos = pos.reshape(m, TOP_K).astype(jnp.int32)
    return pos[:, 0], pos[:, 1], src, tile_expert, tile_valid


def kernel(x_prompt, x_sample, mem_prompt, state_pool, state_swa_k, state_swa_v, cache_diff_k, cache_diff_v,
           cache_mem_k, cache_mem_v, page_table, rel_bias_table, norm_gain, final_gain, w_in_even, pool_map,
           pool_scale, attn_sinks, w_out_even, w_qkv_odd, diff_lambda, diff_subln, w_out_odd, w_mem_q, w_mem_k,
           w_mem_v, w_mem_o, w_ffn_gu, w_ffn_down, w_router, w_exp_gu, w_exp_down):
    _, s, d = x_prompt.shape
    db = x_sample.shape[0]
    n_phys, page = cache_diff_k.shape[:2]
    hp = x_prompt.reshape(s, d)
    hs = x_sample.reshape(db, d)
    mem = mem_prompt.reshape(-1, d)
    ml = mem.shape[0]
    bias_d = _bias_by_distance(rel_bias_table)

    def mem_block(hp, hs, layer):
        mk = _matmul(mem, w_mem_k[layer])
        mv = _matmul(mem, w_mem_v[layer])
        g_mem = norm_gain[layer, 1]
        qp = _matmul(_rmsnorm(hp, g_mem), w_mem_q[layer])
        hp = _matmul(_mem_prompt(qp, mk, mv), w_mem_o[layer], resid=hp)
        qs = _matmul(_rmsnorm(hs, g_mem), w_mem_q[layer])
        os_ = _mem_sample(qs, cache_mem_k[layer].reshape(db, ml, MEM_W), cache_mem_v[layer].reshape(db, ml, MEM_W))
        hs = _matmul(os_, w_mem_o[layer], resid=hs)
        return hp, hs, mk, mv

    g_mix = norm_gain[0, 0]
    up = _matmul(_rmsnorm(hp, g_mix), w_in_even[0])
    us = _matmul(_rmsnorm(hs, g_mix), w_in_even[0])
    a_p = _pool_prompt(up, pool_map[0], pool_scale[0])
    b_p = _swa_prompt(up, bias_d, attn_sinks[0])
    hp = _matmul(jnp.concatenate([a_p, b_p], axis=1), w_out_even[0], resid=hp)

    o1 = POOL_WIDTH
    o2 = o1 + SWA_Q_W
    o3 = o2 + SWA_KV_W
    a_s = _pool_sample(state_pool[0], us, pool_map[0], pool_scale[0])
    k_old = state_swa_k[0].reshape(db, WINDOW, SWA_KV_W)
    v_old = state_swa_v[0].reshape(db, WINDOW, SWA_KV_W)
    k_state = jnp.concatenate([k_old[:, 1:], us[:, None, o2:o3]], axis=1)
    v_state = jnp.concatenate([v_old[:, 1:], us[:, None, o3:]], axis=1)
    b_s = _swa_sample(us[:, o1:o2].reshape(db, SWA_HEADS, SWA_HEAD_DIM), k_state, v_state,
                      k_old[:, :1], v_old[:, :1], bias_d, attn_sinks[0])
    hs = _matmul(jnp.concatenate([a_s, b_s], axis=1), w_out_even[0], resid=hs)

    pool_p = up[s - POOL_STATE:, :o1].reshape(1, 1, POOL_STATE, POOL_WIDTH)
    pool_s = jnp.concatenate([state_pool[0][:, 1:], us[:, None, :o1]], axis=1)[None]
    swk_p = up[s - WINDOW:, o2:o3].reshape(1, 1, WINDOW, SWA_KV_HEADS, SWA_HEAD_DIM)
    swv_p = up[s - WINDOW:, o3:].reshape(1, 1, WINDOW, SWA_KV_HEADS, SWA_HEAD_DIM)
    swk_s = k_state.reshape(1, db, WINDOW, SWA_KV_HEADS, SWA_HEAD_DIM)
    swv_s = v_state.reshape(1, db, WINDOW, SWA_KV_HEADS, SWA_HEAD_DIM)

    hp, hs, mk0, mv0 = mem_block(hp, hs, 0)

    g_ffn = norm_gain[0, 2]
    hp = _ffn(_rmsnorm(hp, g_ffn), w_ffn_gu[0], w_ffn_down[0], hp)
    hs = _ffn(_rmsnorm(hs, g_ffn), w_ffn_gu[0], w_ffn_down[0], hs)

    lam_init = 0.8 - 0.6 * math.exp(-0.3 * 1)
    g_mix = norm_gain[1, 0]
    up = _matmul(_rmsnorm(hp, g_mix), w_qkv_odd[0])
    us = _matmul(_rmsnorm(hs, g_mix), w_qkv_odd[0])
    o_p = _diff_prompt(up, bias_d, rel_bias_table, diff_lambda[0], diff_subln[0], lam_init)
    hp = _matmul(o_p, w_out_odd[0], resid=hp)
    ck = cache_diff_k.reshape(n_phys, page * 8, LANES)
    cv = cache_diff_v.reshape(n_phys, page * 8, LANES)
    o_s = _diff_sample(us, ck, cv, page_table, bias_d, diff_lambda[0], diff_subln[0], lam_init)
    hs = _matmul(o_s, w_out_odd[0], resid=hs)

    q_end = DIFF_Q_W
    k_end = q_end + DIFF_K_W
    dk_p = up[:, q_end:k_end].reshape(1, s, 1, DIFF_KV_HEADS, 2, DIFF_HEAD_DIM)
    dv_p = up[:, k_end:].reshape(1, s, 1, DIFF_KV_HEADS, DIFF_V_DIM)
    dk_s = us[:, q_end:k_end].reshape(db, 1, 1, DIFF_KV_HEADS, 2, DIFF_HEAD_DIM)
    dv_s = us[:, k_end:].reshape(db, 1, 1, DIFF_KV_HEADS, DIFF_V_DIM)

    hp, hs, mk1, mv1 = mem_block(hp, hs, 1)

    h_all = jnp.concatenate([hp, hs], axis=0)
    x_all = _rmsnorm(h_all, norm_gain[1, 2], out_dtype=F32)
    idx, gates = _router(x_all, w_router[0])
    tm = 768
    pos1, pos2, src, tile_expert, tile_valid = _routing_tables(idx[:, :TOP_K], tm)
    x_sorted = _gather_rows(x_all, src)
    y_sorted = _moe_experts(x_sorted, tile_expert, tile_valid, w_exp_gu[0], w_exp_down[0], tm)
    y_all = _moe_combine_norm(y_sorted, pos1, pos2, gates, h_all, final_gain)

    mem_shape = (1, ml, MEM_HEADS, MEM_HEAD_DIM)
    return (y_all[:s].reshape(1, s, d), y_all[s:].reshape(db, 1, d),
            pool_p, pool_s, swk_p, swv_p, swk_s, swv_s,
            dk_p, dv_p, dk_s, dv_s,
            jnp.stack([mk0.reshape(mem_shape), mk1.reshape(mem_shape)]),
            jnp.stack([mv0.reshape(mem_shape), mv1.reshape(mem_shape)]))
```

```python
import functools
import math

import numpy as np
import jax
import jax.numpy as jnp
from jax import lax
from jax.experimental import pallas as pl
from jax.experimental.pallas import tpu as pltpu

F32 = jnp.float32
BF16 = jnp.bfloat16
EPS = 1e-6

REL_BUCKETS = 32
REL_MAX_EXACT = 16
REL_MAX_DIST = 128
REL_HEADS = 16
POOL_WINDOWS = (2, 4, 8, 16)
POOL_WIDTH = 1024
POOL_GROUP = POOL_WIDTH // len(POOL_WINDOWS)
POOL_STATE = max(POOL_WINDOWS) - 1
WINDOW = 128
SWA_HEADS = 16
SWA_KV_HEADS = 4
SWA_HEAD_DIM = 64
SWA_GROUP = SWA_HEADS // SWA_KV_HEADS
SWA_Q_W = SWA_HEADS * SWA_HEAD_DIM
SWA_KV_W = SWA_KV_HEADS * SWA_HEAD_DIM
DIFF_KV_HEADS = 4
DIFF_GROUP = 2
DIFF_HEAD_DIM = 128
DIFF_V_DIM = 256
DIFF_MAPS = DIFF_KV_HEADS * DIFF_GROUP * 2
DIFF_Q_W = DIFF_MAPS * DIFF_HEAD_DIM
DIFF_K_W = DIFF_KV_HEADS * 2 * DIFF_HEAD_DIM
DIFF_V_W = DIFF_KV_HEADS * DIFF_V_DIM
DIFF_OUT_W = DIFF_KV_HEADS * DIFF_GROUP * DIFF_V_DIM
MEM_HEADS = 4
MEM_HEAD_DIM = 128
MEM_W = MEM_HEADS * MEM_HEAD_DIM
N_EXPERTS = 8
TOP_K = 2

ANY_DIST = 2 ** 30
LANES = 128
VMEM_LIMIT = 56 * 1024 * 1024


def _params(*sem):
    return pltpu.CompilerParams(dimension_semantics=sem, vmem_limit_bytes=VMEM_LIMIT)


def _tile(n, pref):
    if n <= pref:
        return n
    t = pref
    while n % t:
        t -= 8
    return t


def _dot(a, b):
    return jnp.dot(a, b, preferred_element_type=F32)


def _dot_nt(a, b):
    return lax.dot_general(a, b, (((1,), (1,)), ((), ())), preferred_element_type=F32)


def _rmsnorm_kernel(x_ref, g_ref, o_ref):
    x = x_ref[...]
    y = x * lax.rsqrt(jnp.mean(x * x, axis=-1, keepdims=True) + EPS)
    o_ref[...] = (y * g_ref[...]).astype(o_ref.dtype)


def _rmsnorm(x, gain, out_dtype=BF16):
    m, d = x.shape
    tr = _tile(m, 512)
    return pl.pallas_call(
        _rmsnorm_kernel,
        out_shape=jax.ShapeDtypeStruct((m, d), out_dtype),
        grid=(m // tr,),
        in_specs=[pl.BlockSpec((tr, d), lambda i: (i, 0)),
                  pl.BlockSpec((1, d), lambda i: (0, 0))],
        out_specs=pl.BlockSpec((tr, d), lambda i: (i, 0)),
        compiler_params=_params("parallel"),
        name="rmsnorm",
    )(x, gain.reshape(1, d))


def _mm_kernel(*refs, has_resid):
    if has_resid:
        a_ref, w_ref, r_ref, o_ref = refs
    else:
        a_ref, w_ref, o_ref = refs
    acc = _dot(a_ref[...].astype(BF16), w_ref[...].astype(BF16))
    if has_resid:
        acc = r_ref[...] + acc
    o_ref[...] = acc.astype(o_ref.dtype)


def _matmul(a, w, resid=None, tm_pref=1024, tn_pref=512):
    m, k = a.shape
    n = w.shape[1]
    tm = _tile(m, tm_pref)
    tn = _tile(n, tn_pref)
    in_specs = [pl.BlockSpec((tm, k), lambda i, j: (i, 0)),
                pl.BlockSpec((k, tn), lambda i, j: (0, j))]
    args = [a, w]
    if resid is not None:
        in_specs.append(pl.BlockSpec((tm, tn), lambda i, j: (i, j)))
        args.append(resid)
    return pl.pallas_call(
        functools.partial(_mm_kernel, has_resid=resid is not None),
        out_shape=jax.ShapeDtypeStruct((m, n), F32),
        grid=(m // tm, n // tn),
        in_specs=in_specs,
        out_specs=pl.BlockSpec((tm, tn), lambda i, j: (i, j)),
        compiler_params=_params("parallel", "parallel"),
        name="matmul",
    )(*args)


def _swiglu_step(x, wg_ref, wu_ref, wd_ref):
    g = _dot(x, wg_ref[...].astype(BF16))
    u = _dot(x, wu_ref[...].astype(BF16))
    h = (g * jax.nn.sigmoid(g) * u).astype(BF16)
    return _dot(h, wd_ref[...].astype(BF16))


def _ffn_kernel(x_ref, wg_ref, wu_ref, wd_ref, r_ref, o_ref):
    @pl.when(pl.program_id(1) == 0)
    def _():
        o_ref[...] = r_ref[...]

    o_ref[...] += _swiglu_step(x_ref[...], wg_ref, wu_ref, wd_ref)


def _ffn(x, w_gu, w_down, resid, tm_pref=1024, tf_pref=256):
    m, d = x.shape
    f = w_down.shape[0]
    tm = _tile(m, tm_pref)
    tf = _tile(f, tf_pref)
    nf = f // tf
    return pl.pallas_call(
        _ffn_kernel,
        out_shape=jax.ShapeDtypeStruct((m, d), F32),
        grid=(m // tm, nf),
        in_specs=[pl.BlockSpec((tm, d), lambda i, j: (i, 0)),
                  pl.BlockSpec((d, tf), lambda i, j: (0, j)),
                  pl.BlockSpec((d, tf), lambda i, j: (0, j + nf)),
                  pl.BlockSpec((tf, d), lambda i, j: (j, 0)),
                  pl.BlockSpec((tm, d), lambda i, j: (i, 0))],
        out_specs=pl.BlockSpec((tm, d), lambda i, j: (i, 0)),
        compiler_params=_params("parallel", "arbitrary"),
        name="ffn_swiglu",
    )(x, w_gu, w_gu, w_down, resid)


def _moe_kernel(te_ref, tv_ref, x_ref, wg_ref, wu_ref, wd_ref, o_ref):
    i = pl.program_id(0)

    @pl.when(tv_ref[i] > 0)
    def _():
        y = _swiglu_step(x_ref[...].astype(BF16), wg_ref, wu_ref, wd_ref)

        @pl.when(pl.program_id(1) == 0)
        def _():
            o_ref[...] = y

        @pl.when(pl.program_id(1) > 0)
        def _():
            o_ref[...] += y


def _moe_experts(x_sorted, tile_expert, tile_valid, w_gu, w_down, tm, tf_pref=256):
    np_rows, d = x_sorted.shape
    f = w_down.shape[1]
    tf = _tile(f, tf_pref)
    nf = f // tf

    def chunk(j, tv, i):
        return jnp.where(tv[i] > 0, j, nf - 1)

    grid_spec = pltpu.PrefetchScalarGridSpec(
        num_scalar_prefetch=2,
        grid=(np_rows // tm, nf),
        in_specs=[pl.BlockSpec((tm, d), lambda i, j, te, tv: (i, 0)),
                  pl.BlockSpec((None, d, tf), lambda i, j, te, tv: (te[i], 0, chunk(j, tv, i))),
                  pl.BlockSpec((None, d, tf), lambda i, j, te, tv: (te[i], 0, chunk(j, tv, i) + nf)),
                  pl.BlockSpec((None, tf, d), lambda i, j, te, tv: (te[i], chunk(j, tv, i), 0))],
        out_specs=pl.BlockSpec((tm, d), lambda i, j, te, tv: (i, 0)),
    )
    return pl.pallas_call(
        _moe_kernel,
        out_shape=jax.ShapeDtypeStruct((np_rows, d), F32),
        grid_spec=grid_spec,
        compiler_params=_params("arbitrary", "arbitrary"),
        name="moe_experts",
    )(tile_expert, tile_valid, x_sorted, w_gu, w_gu, w_down)


def _router_kernel(x_ref, w_ref, idx_ref, gate_ref):
    lg = _dot(x_ref[...].astype(BF16), w_ref[...].astype(BF16))
    col = lax.broadcasted_iota(jnp.int32, lg.shape, 1)
    lg = jnp.where(col < N_EXPERTS, lg, -jnp.inf)
    v1 = jnp.max(lg, axis=-1, keepdims=True)
    i1 = jnp.min(jnp.where(lg == v1, col, LANES), axis=-1, keepdims=True)
    lg2 = jnp.where(col == i1, -jnp.inf, lg)
    v2 = jnp.max(lg2, axis=-1, keepdims=True)
    i2 = jnp.min(jnp.where(lg2 == v2, col, LANES), axis=-1, keepdims=True)
    e2 = jnp.exp(v2 - v1)
    den = 1.0 + e2
    idx_ref[...] = jnp.where(col == 0, i1, jnp.where(col == 1, i2, 0))
    gate_ref[...] = jnp.where(col == 0, 1.0 / den, jnp.where(col == 1, e2 / den, 0.0))


def _router(x, w_router):
    m, d = x.shape
    w_pad = jnp.zeros((d, LANES), F32).at[:, :N_EXPERTS].set(w_router)
    tm = _tile(m, 640)
    return pl.pallas_call(
        _router_kernel,
        out_shape=(jax.ShapeDtypeStruct((m, LANES), jnp.int32),
                   jax.ShapeDtypeStruct((m, LANES), F32)),
        grid=(m // tm,),
        in_specs=[pl.BlockSpec((tm, d), lambda i: (i, 0)),
                  pl.BlockSpec((d, LANES), lambda i: (0, 0))],
        out_specs=(pl.BlockSpec((tm, LANES), lambda i: (i, 0)),
                   pl.BlockSpec((tm, LANES), lambda i: (i, 0))),
        compiler_params=_params("parallel"),
        name="router_top2",
    )(x, w_pad)


def _row_copy(src_hbm, row, dst_ref, r, sem):
    return pltpu.make_async_copy(src_hbm.at[pl.ds(row, 1)], dst_ref.at[pl.ds(r, 1)], sem)


def _gather_kernel(src_ref, x_hbm, o_ref, sem, *, tg):
    base = pl.program_id(0) * tg

    def issue(r, c):
        _row_copy(x_hbm, src_ref[base + r], o_ref, r, sem).start()
        return c

    lax.fori_loop(0, tg, issue, 0, unroll=8)

    def drain(r, c):
        _row_copy(x_hbm, 0, o_ref, r, sem).wait()
        return c

    lax.fori_loop(0, tg, drain, 0, unroll=8)


def _gather_rows(x, src, tg=256):
    n = src.shape[0]
    d = x.shape[1]
    grid_spec = pltpu.PrefetchScalarGridSpec(
        num_scalar_prefetch=1,
        grid=(n // tg,),
        in_specs=[pl.BlockSpec(memory_space=pl.ANY)],
        out_specs=pl.BlockSpec((tg, d), lambda i, s: (i, 0)),
        scratch_shapes=[pltpu.SemaphoreType.DMA(())],
    )
    return pl.pallas_call(
        functools.partial(_gather_kernel, tg=tg),
        out_shape=jax.ShapeDtypeStruct((n, d), x.dtype),
        grid_spec=grid_spec,
        compiler_params=_params("arbitrary"),
        name="gather_rows",
    )(src, x)


def _combine_kernel(p1_ref, p2_ref, y_hbm, gate_ref, r_ref, fg_ref, o_ref, buf, sem, *, tc):
    base = pl.program_id(0) * tc

    def issue(r, c):
        _row_copy(y_hbm, p1_ref[base + r], buf.at[0], r, sem).start()
        _row_copy(y_hbm, p2_ref[base + r], buf.at[1], r, sem).start()
        return c

    lax.fori_loop(0, tc, issue, 0, unroll=8)

    def drain(r, c):
        _row_copy(y_hbm, 0, buf.at[0], r, sem).wait()
        _row_copy(y_hbm, 0, buf.at[1], r, sem).wait()
        return c

    lax.fori_loop(0, tc, drain, 0, unroll=8)

    g = gate_ref[...]
    h = r_ref[...] + (buf[0] * g[:, 0:1] + buf[1] * g[:, 1:2])
    y = h * lax.rsqrt(jnp.mean(h * h, axis=-1, keepdims=True) + EPS)
    o_ref[...] = y * fg_ref[...]


def _moe_combine_norm(y_sorted, pos1, pos2, gates, resid, final_gain, tc=128):
    m, d = resid.shape
    grid_spec = pltpu.PrefetchScalarGridSpec(
        num_scalar_prefetch=2,
        grid=(m // tc,),
        in_specs=[pl.BlockSpec(memory_space=pl.ANY),
                  pl.BlockSpec((tc, LANES), lambda i, a, b: (i, 0)),
                  pl.BlockSpec((tc, d), lambda i, a, b: (i, 0)),
                  pl.BlockSpec((1, d), lambda i, a, b: (0, 0))],
        out_specs=pl.BlockSpec((tc, d), lambda i, a, b: (i, 0)),
        scratch_shapes=[pltpu.VMEM((2, tc, d), F32), pltpu.SemaphoreType.DMA(())],
    )
    return pl.pallas_call(
        functools.partial(_combine_kernel, tc=tc),
        out_shape=jax.ShapeDtypeStruct((m, d), F32),
        grid_spec=grid_spec,
        compiler_params=_params("arbitrary"),
        name="moe_combine_norm",
    )(pos1, pos2, y_sorted, gates, resid, final_gain.reshape(1, d))


def _bucket_thresholds():
    n = np.arange(REL_MAX_EXACT, 4 * REL_MAX_DIST, dtype=np.int64)
    ratio = np.log(n.astype(np.float32) / np.float32(REL_MAX_EXACT))
    large = REL_MAX_EXACT + (ratio / np.float32(math.log(REL_MAX_DIST / REL_MAX_EXACT))
                             * np.float32(REL_BUCKETS - REL_MAX_EXACT)).astype(np.int32)
    large = np.minimum(large, REL_BUCKETS - 1)
    return [int(n[np.argmax(large >= b)]) for b in range(REL_MAX_EXACT + 1, REL_BUCKETS)]


def _bias_kernel(t_ref, o_ref, *, off, lo, hi, thresholds):
    h = pl.program_id(0)
    rc, cols = o_ref.shape
    i = pl.program_id(1) * rc + lax.broadcasted_iota(jnp.int32, (rc, cols), 0)
    j = lax.broadcasted_iota(jnp.int32, (rc, cols), 1)
    dist = off + i - j
    n = jnp.maximum(dist, 0)
    bucket = jnp.minimum(n, REL_MAX_EXACT)
    for t in thresholds:
        bucket = bucket + (n >= t).astype(jnp.int32)
    acc = jnp.full((rc, cols), t_ref[REL_BUCKETS - 1, h], F32)
    for b in range(REL_BUCKETS - 1):
        acc = jnp.where(bucket == b, t_ref[b, h], acc)
    o_ref[...] = jnp.where((dist >= lo) & (dist <= hi), acc, -jnp.inf)


def _bias_tiles(rel_table, rows, cols, off, lo, hi):
    rc = _tile(rows, 64)
    return pl.pallas_call(
        functools.partial(_bias_kernel, off=off, lo=lo, hi=hi, thresholds=_bucket_thresholds()),
        out_shape=jax.ShapeDtypeStruct((REL_HEADS, rows, cols), F32),
        grid=(REL_HEADS, rows // rc),
        in_specs=[pl.BlockSpec(memory_space=pltpu.SMEM)],
        out_specs=pl.BlockSpec((None, rc, cols), lambda h, r: (h, r, 0)),
        compiler_params=_params("parallel", "parallel"),
        name="rel_bias_tiles",
    )(rel_table)


def _pool_project(pooled, map_ref, scale_ref, o_ref):
    for g in range(len(POOL_WINDOWS)):
        cols = slice(g * POOL_GROUP, (g + 1) * POOL_GROUP)
        mixed = _dot(pooled[g].astype(BF16), map_ref[g].astype(BF16))
        o_ref[:, cols] = (mixed * scale_ref[:, cols]).astype(o_ref.dtype)


def _pool_prompt_kernel(u_ref, halo_ref, map_ref, scale_ref, o_ref, *, tp):
    i = pl.program_id(0)
    halo_rows = halo_ref.shape[0]
    halo = jnp.where(i > 0, halo_ref[...], 0.0)
    ext = jnp.concatenate([halo, u_ref[...]], axis=0)
    row = i * tp + lax.broadcasted_iota(jnp.int32, (tp, 1), 0)
    pooled = []
    for g, w in enumerate(POOL_WINDOWS):
        cols = slice(g * POOL_GROUP, (g + 1) * POOL_GROUP)
        s = ext[:, cols]
        shift = 1
        while shift < w:
            s = s + pltpu.roll(s, shift, axis=0)
            shift *= 2
        cnt = jnp.minimum(row + 1, w).astype(F32)
        x = ext[halo_rows:, cols]
        pooled.append(s[halo_rows:] / cnt - x)
    _pool_project(pooled, map_ref, scale_ref, o_ref)


def _pool_prompt(u, pool_map, pool_scale, tp=256):
    s = u.shape[0]
    tp = _tile(s, tp)
    halo = 16
    return pl.pallas_call(
        functools.partial(_pool_prompt_kernel, tp=tp),
        out_shape=jax.ShapeDtypeStruct((s, POOL_WIDTH), BF16),
        grid=(s // tp,),
        in_specs=[pl.BlockSpec((tp, POOL_WIDTH), lambda i: (i, 0)),
                  pl.BlockSpec((halo, POOL_WIDTH), lambda i: (jnp.maximum(i * (tp // halo) - 1, 0), 0)),
                  pl.BlockSpec(pool_map.shape, lambda i: (0, 0, 0)),
                  pl.BlockSpec((1, POOL_WIDTH), lambda i: (0, 0))],
        out_specs=pl.BlockSpec((tp, POOL_WIDTH), lambda i: (i, 0)),
        compiler_params=_params("parallel"),
        name="pool_prompt",
    )(u, u, pool_map, pool_scale.reshape(1, POOL_WIDTH))


def _pool_sample_kernel(st_ref, u_ref, map_ref, scale_ref, o_ref):
    pooled = []
    for g, w in enumerate(POOL_WINDOWS):
        cols = slice(g * POOL_GROUP, (g + 1) * POOL_GROUP)
        x = u_ref[:, cols]
        s = x
        for k in range(1, w):
            s = s + st_ref[:, POOL_STATE - k, cols]
        pooled.append(s / float(w) - x)
    _pool_project(pooled, map_ref, scale_ref, o_ref)


def _pool_sample(state, u, pool_map, pool_scale):
    db = state.shape[0]
    return pl.pallas_call(
        _pool_sample_kernel,
        out_shape=jax.ShapeDtypeStruct((db, POOL_WIDTH), BF16),
        grid=(1,),
        in_specs=[pl.BlockSpec(state.shape, lambda i: (0, 0, 0)),
                  pl.BlockSpec((db, POOL_WIDTH), lambda i: (0, 0)),
                  pl.BlockSpec(pool_map.shape, lambda i: (0, 0, 0)),
                  pl.BlockSpec((1, POOL_WIDTH), lambda i: (0, 0))],
        out_specs=pl.BlockSpec((db, POOL_WIDTH), lambda i: (0, 0)),
        compiler_params=_params("arbitrary"),
        name="pool_sample",
    )(state, u, pool_map, pool_scale.reshape(1, POOL_WIDTH))


def _swa_prompt_kernel(q_ref, kp_ref, kc_ref, vp_ref, vc_ref, bias_ref, sink_ref, o_ref):
    i = pl.program_id(0)
    col = lax.broadcasted_iota(jnp.int32, (WINDOW, 2 * WINDOW), 1)
    key_ok = (i > 0) | (col >= WINDOW)
    k_all = jnp.concatenate([kp_ref[...], kc_ref[...]], axis=0).astype(BF16)
    v_all = jnp.concatenate([vp_ref[...], vc_ref[...]], axis=0).astype(BF16)
    scale = SWA_HEAD_DIM ** -0.5
    for h in range(SWA_KV_HEADS):
        kv_cols = slice(h * SWA_HEAD_DIM, (h + 1) * SWA_HEAD_DIM)
        k_h = k_all[:, kv_cols]
        v_h = v_all[:, kv_cols]
        for g in range(SWA_GROUP):
            head = h * SWA_GROUP + g
            q_cols = slice(head * SWA_HEAD_DIM, (head + 1) * SWA_HEAD_DIM)
            s = _dot_nt(q_ref[:, q_cols].astype(BF16), k_h) * scale + bias_ref[head]
            s = jnp.where(key_ok, s, -jnp.inf)
            sink = sink_ref[head]
            m = jnp.maximum(jnp.max(s, axis=-1, keepdims=True), sink)
            p = jnp.exp(s - m)
            p = p / (jnp.sum(p, axis=-1, keepdims=True) + jnp.exp(sink - m))
            o_ref[:, q_cols] = _dot(p.astype(BF16), v_h).astype(o_ref.dtype)


def _swa_bias(rel_table):
    return _bias_tiles(rel_table, WINDOW, 2 * WINDOW, WINDOW, 0, WINDOW)


def _swa_prompt(u, bias, sinks):
    s = u.shape[0]
    nb = s // WINDOW
    qb = POOL_WIDTH // SWA_Q_W
    kb = (POOL_WIDTH + SWA_Q_W) // SWA_KV_W
    prev = lambda i: (jnp.maximum(i - 1, 0), kb)
    return pl.pallas_call(
        _swa_prompt_kernel,
        out_shape=jax.ShapeDtypeStruct((s, SWA_Q_W), BF16),
        grid=(nb,),
        in_specs=[pl.BlockSpec((WINDOW, SWA_Q_W), lambda i: (i, qb)),
                  pl.BlockSpec((WINDOW, SWA_KV_W), prev),
                  pl.BlockSpec((WINDOW, SWA_KV_W), lambda i: (i, kb)),
                  pl.BlockSpec((WINDOW, SWA_KV_W), lambda i: (jnp.maximum(i - 1, 0), kb + 1)),
                  pl.BlockSpec((WINDOW, SWA_KV_W), lambda i: (i, kb + 1)),
                  pl.BlockSpec(bias.shape, lambda i: (0, 0, 0)),
                  pl.BlockSpec(memory_space=pltpu.SMEM)],
        out_specs=pl.BlockSpec((WINDOW, SWA_Q_W), lambda i: (i, 0)),
        compiler_params=_params("parallel"),
        name="swa_prompt",
    )(u, u, u, u, u, bias, sinks)


def _swa_sample_kernel(q_ref, k_ref, v_ref, k0_ref, v0_ref, bias_ref, bias0_ref, sink_ref, o_ref):
    scale = SWA_HEAD_DIM ** -0.5
    lane = lax.broadcasted_iota(jnp.int32, (1, SWA_HEADS, SWA_KV_W), 2)
    row = lax.broadcasted_iota(jnp.int32, (1, SWA_HEADS, SWA_KV_W), 1)
    own = (lane // SWA_HEAD_DIM) == (row // SWA_GROUP)
    q_bd = jnp.where(own, q_ref[...], 0.0).astype(BF16)
    k = k_ref[...].astype(BF16)
    v = v_ref[...].astype(BF16)
    s = jnp.einsum('bhc,bkc->bhk', q_bd, k, preferred_element_type=F32) * scale + bias_ref[...][None]
    k0 = k0_ref[...].astype(BF16).astype(F32)
    s0 = jnp.sum(q_bd.astype(F32) * k0, axis=-1, keepdims=True) * scale + bias0_ref[...][None]
    sink = sink_ref[...][None]
    m = jnp.maximum(jnp.maximum(jnp.max(s, axis=-1, keepdims=True), s0), sink)
    p = jnp.exp(s - m)
    p0 = jnp.exp(s0 - m)
    den = jnp.sum(p, axis=-1, keepdims=True) + p0 + jnp.exp(sink - m)
    o = jnp.einsum('bhk,bkc->bhc', (p / den).astype(BF16), v, preferred_element_type=F32)
    o = o + (p0 / den).astype(BF16).astype(F32) * v0_ref[...].astype(BF16).astype(F32)
    for h in range(SWA_KV_HEADS):
        rows = slice(h * SWA_GROUP, (h + 1) * SWA_GROUP)
        o_ref[:, rows, :] = o[:, rows, h * SWA_HEAD_DIM:(h + 1) * SWA_HEAD_DIM].astype(o_ref.dtype)


def _swa_sample(q, k_state, v_state, k0, v0, bias_swa, sinks, sb=8):
    db = q.shape[0]
    sb = _tile(db, sb)
    q_t = jnp.tile(q, (1, 1, SWA_KV_HEADS))
    bias = bias_swa[:, 0, 1:WINDOW + 1]
    bias0 = bias_swa[:, 0, 0:1]
    blk3 = lambda a, b: pl.BlockSpec((sb, a, b), lambda i: (i, 0, 0))
    o = pl.pallas_call(
        _swa_sample_kernel,
        out_shape=jax.ShapeDtypeStruct((db, SWA_HEADS, SWA_HEAD_DIM), BF16),
        grid=(db // sb,),
        in_specs=[blk3(SWA_HEADS, SWA_KV_W), blk3(WINDOW, SWA_KV_W), blk3(WINDOW, SWA_KV_W),
                  blk3(1, SWA_KV_W), blk3(1, SWA_KV_W),
                  pl.BlockSpec((SWA_HEADS, WINDOW), lambda i: (0, 0)),
                  pl.BlockSpec((SWA_HEADS, 1), lambda i: (0, 0)),
                  pl.BlockSpec((SWA_HEADS, 1), lambda i: (0, 0))],
        out_specs=blk3(SWA_HEADS, SWA_HEAD_DIM),
        compiler_params=_params("parallel"),
        name="swa_sample",
    )(q_t, k_state, v_state, k0, v0, bias, bias0, sinks.reshape(SWA_HEADS, 1))
    return o.reshape(db, SWA_Q_W)


def _mem_prompt_kernel(q_ref, k_ref, v_ref, o_ref):
    scale = MEM_HEAD_DIM ** -0.5
    for h in range(MEM_HEADS):
        cols = slice(h * MEM_HEAD_DIM, (h + 1) * MEM_HEAD_DIM)
        s = _dot_nt(q_ref[:, cols].astype(BF16), k_ref[:, cols].astype(BF16)) * scale
        p = jnp.exp(s - jnp.max(s, axis=-1, keepdims=True))
        p = p / jnp.sum(p, axis=-1, keepdims=True)
        o_ref[:, cols] = _dot(p.astype(BF16), v_ref[:, cols].astype(BF16)).astype(o_ref.dtype)


def _mem_prompt(q, mk, mv, tq=512):
    s = q.shape[0]
    tq = _tile(s, tq)
    return pl.pallas_call(
        _mem_prompt_kernel,
        out_shape=jax.ShapeDtypeStruct((s, MEM_W), BF16),
        grid=(s // tq,),
        in_specs=[pl.BlockSpec((tq, MEM_W), lambda i: (i, 0)),
                  pl.BlockSpec(mk.shape, lambda i: (0, 0)),
                  pl.BlockSpec(mv.shape, lambda i: (0, 0))],
        out_specs=pl.BlockSpec((tq, MEM_W), lambda i: (i, 0)),
        compiler_params=_params("parallel"),
        name="mem_attn_prompt",
    )(q, mk, mv)


def _mem_sample_kernel(q_ref, k_ref, v_ref, o_ref):
    scale = MEM_HEAD_DIM ** -0.5
    lane = lax.broadcasted_iota(jnp.int32, (1, MEM_HEADS, MEM_W), 2)
    row = lax.broadcasted_iota(jnp.int32, (1, MEM_HEADS, MEM_W), 1)
    own = (lane // MEM_HEAD_DIM) == row
    q_bd = jnp.where(own, q_ref[...], 0.0).astype(BF16)
    ml = k_ref.shape[1] // MEM_HEADS

    def matrix(ref):
        return jnp.concatenate([ref[:, pl.ds(hh, ml, stride=MEM_HEADS), :] for hh in range(MEM_HEADS)],
                               axis=2).astype(BF16)

    s = jnp.einsum('bhc,bkc->bhk', q_bd, matrix(k_ref), preferred_element_type=F32) * scale
    p = jnp.exp(s - jnp.max(s, axis=-1, keepdims=True))
    p = p / jnp.sum(p, axis=-1, keepdims=True)
    o = jnp.einsum('bhk,bkc->bhc', p.astype(BF16), matrix(v_ref), preferred_element_type=F32)
    o_ref[...] = jnp.sum(jnp.where(own, o, 0.0), axis=1, keepdims=True).astype(o_ref.dtype)


def _mem_sample(q, mem_k, mem_v, layer, sb=8):
    _, db, rows, _ = mem_k.shape
    sb = _tile(db, sb)
    kv_spec = pl.BlockSpec((None, sb, rows, MEM_HEAD_DIM), lambda i: (layer, i, 0, 0))
    o = pl.pallas_call(
        _mem_sample_kernel,
        out_shape=jax.ShapeDtypeStruct((db, 1, MEM_W), BF16),
        grid=(db // sb,),
        in_specs=[pl.BlockSpec((sb, 1, MEM_W), lambda i: (i, 0, 0)), kv_spec, kv_spec],
        out_specs=pl.BlockSpec((sb, 1, MEM_W), lambda i: (i, 0, 0)),
        compiler_params=_params("parallel"),
        name="mem_attn_sample",
    )(q.reshape(db, 1, MEM_W), mem_k, mem_v)
    return o.reshape(db, MEM_W)


def _lambda_value(lam_ref, lam_init):
    lp = lam_ref[...]
    a = jnp.sum(lp[0:1] * lp[1:2], axis=-1, keepdims=True)
    b = jnp.sum(lp[2:3] * lp[3:4], axis=-1, keepdims=True)
    return jnp.exp(a) - jnp.exp(b) + lam_init


def _diff_finish(o0, o1, lam, subln, lam_init):
    o = o0 - lam * o1
    o = o * lax.rsqrt(jnp.mean(o * o, axis=-1, keepdims=True) + EPS)
    return o * subln * (1.0 - lam_init)


def _diff_prompt_kernel(qi_ref, kj_ref, q_ref, k_ref, v_ref, bias_ref, far_ref, lam_ref, subln_ref,
                        o_ref, m_sc, l_sc, acc_sc, *, lam_init):
    h = pl.program_id(0)
    step = pl.program_id(1)
    qi = qi_ref[step]
    kj = kj_ref[step]
    scale = DIFF_HEAD_DIM ** -0.5
    combos = [(g, mp) for g in range(DIFF_GROUP) for mp in range(2)]

    @pl.when(kj == 0)
    def _():
        m_sc[...] = jnp.full(m_sc.shape, -jnp.inf, F32)
        l_sc[...] = jnp.zeros(l_sc.shape, F32)
        acc_sc[...] = jnp.zeros(acc_sc.shape, F32)

    def update(bias_of):
        v = v_ref[...].astype(BF16)
        for c, (g, mp) in enumerate(combos):
            q = q_ref[:, pl.ds((g * 2 + mp) * DIFF_HEAD_DIM, DIFF_HEAD_DIM)].astype(BF16)
            k = k_ref[:, pl.ds(mp * DIFF_HEAD_DIM, DIFF_HEAD_DIM)].astype(BF16)
            s = _dot_nt(q, k) * scale + bias_of(c)
            m_old = m_sc[c]
            m_new = jnp.maximum(m_old, jnp.max(s, axis=-1, keepdims=True))
            p = jnp.exp(s - m_new)
            alpha = jnp.exp(m_old - m_new)
            l_sc[c] = alpha * l_sc[c] + jnp.sum(p, axis=-1, keepdims=True)
            acc_sc[c] = alpha * acc_sc[c] + _dot(p.astype(BF16), v)
            m_sc[c] = m_new

    @pl.when(kj < qi - 1)
    def _():
        update(lambda c: far_ref[h * 4 + c])

    @pl.when(kj == qi - 1)
    def _():
        update(lambda c: bias_ref[1, c])

    @pl.when(kj == qi)
    def _():
        update(lambda c: bias_ref[0, c])
        lam = _lambda_value(lam_ref, lam_init)
        for g in range(DIFF_GROUP):
            o0 = acc_sc[2 * g] / l_sc[2 * g]
            o1 = acc_sc[2 * g + 1] / l_sc[2 * g + 1]
            o_ref[:, pl.ds(g * DIFF_V_DIM, DIFF_V_DIM)] = _diff_finish(
                o0, o1, lam, subln_ref[...], lam_init).astype(o_ref.dtype)


def _diff_prompt(u, rel_table, lam_params, subln, lam_init, t_pref=512):
    s = u.shape[0]
    t = _tile(s, t_pref)
    nq = s // t
    pairs = [(qi, kj) for qi in range(nq) for kj in range(qi + 1)]
    qi_arr = jnp.asarray([p[0] for p in pairs], jnp.int32)
    kj_arr = jnp.asarray([p[1] for p in pairs], jnp.int32)
    diag = _bias_tiles(rel_table, t, t, 0, 0, ANY_DIST)
    sub = _bias_tiles(rel_table, t, t, t, 0, ANY_DIST)
    bias = jnp.stack([diag, sub]).reshape(2, DIFF_KV_HEADS, 4, t, t)
    far = rel_table[REL_BUCKETS - 1]
    kb = DIFF_Q_W // (2 * DIFF_HEAD_DIM)
    vb = (DIFF_Q_W + DIFF_K_W) // DIFF_V_DIM
    grid_spec = pltpu.PrefetchScalarGridSpec(
        num_scalar_prefetch=2,
        grid=(DIFF_KV_HEADS, len(pairs)),
        in_specs=[pl.BlockSpec((t, 4 * DIFF_HEAD_DIM), lambda h, st, qi, kj: (qi[st], h)),
                  pl.BlockSpec((t, 2 * DIFF_HEAD_DIM), lambda h, st, qi, kj: (kj[st], kb + h)),
                  pl.BlockSpec((t, DIFF_V_DIM), lambda h, st, qi, kj: (kj[st], vb + h)),
                  pl.BlockSpec((2, None, 4, t, t), lambda h, st, qi, kj: (0, h, 0, 0, 0)),
                  pl.BlockSpec(memory_space=pltpu.SMEM),
                  pl.BlockSpec((4, DIFF_HEAD_DIM), lambda h, st, qi, kj: (0, 0)),
                  pl.BlockSpec((1, DIFF_V_DIM), lambda h, st, qi, kj: (0, 0))],
        out_specs=pl.BlockSpec((t, DIFF_GROUP * DIFF_V_DIM), lambda h, st, qi, kj: (qi[st], h)),
        scratch_shapes=[pltpu.VMEM((4, t, 1), F32), pltpu.VMEM((4, t, 1), F32),
                        pltpu.VMEM((4, t, DIFF_V_DIM), F32)],
    )
    assert t >= REL_MAX_DIST, "far blocks must lie entirely in the last distance bucket"
    return pl.pallas_call(
        functools.partial(_diff_prompt_kernel, lam_init=lam_init),
        out_shape=jax.ShapeDtypeStruct((s, DIFF_OUT_W), BF16),
        grid_spec=grid_spec,
        compiler_params=_params("arbitrary", "arbitrary"),
        name="diff_attn_prompt",
    )(qi_arr, kj_arr, u, u, u, bias, far, lam_params, subln.reshape(1, DIFF_V_DIM))


def _diff_sample_kernel(pt_ref, q_ref, kn_ref, vn_ref, bias_ref, bias0_ref, lam_ref, subln_ref, *rest,
                        pages, lam_init):
    k_refs = rest[:pages]
    v_refs = rest[pages:2 * pages]
    o_ref, m_sc, l_sc, acc_sc = rest[2 * pages:]
    c = pl.program_id(1)
    scale = DIFF_HEAD_DIM ** -0.5
    page = k_refs[0].shape[0] // 8
    kw = DIFF_K_W
    lane = lax.broadcasted_iota(jnp.int32, (DIFF_MAPS, kw), 1)
    row = lax.broadcasted_iota(jnp.int32, (DIFF_MAPS, kw), 0)
    q_bd = jnp.where((lane // DIFF_HEAD_DIM) == (row // DIFF_GROUP), q_ref[...], 0.0).astype(BF16)

    @pl.when(c == 0)
    def _():
        kn = kn_ref[...].astype(BF16).astype(F32)
        s0 = jnp.sum(q_bd.astype(F32) * kn, axis=-1, keepdims=True) * scale + bias0_ref[...]
        m_sc[...] = s0
        l_sc[...] = jnp.ones(l_sc.shape, F32)
        acc_sc[...] = jnp.broadcast_to(vn_ref[...].astype(BF16).astype(F32), acc_sc.shape)

    def matrix(ref, starts):
        return jnp.concatenate([ref[pl.ds(r, page, stride=8), :] for r in starts], axis=1).astype(BF16)

    k_rows = list(range(8))
    v_rows = [half * DIFF_KV_HEADS + hh for hh in range(DIFF_KV_HEADS) for half in range(2)]
    s = jnp.concatenate([_dot_nt(q_bd, matrix(k_refs[i], k_rows)) for i in range(pages)], axis=1)
    s = s * scale + bias_ref[:, pl.ds(pl.multiple_of(c * (pages * page), LANES), pages * page)]
    m_old = m_sc[...]
    m_new = jnp.maximum(m_old, jnp.max(s, axis=-1, keepdims=True))
    p = jnp.exp(s - m_new)
    alpha = jnp.exp(m_old - m_new)
    l_sc[...] = alpha * l_sc[...] + jnp.sum(p, axis=-1, keepdims=True)
    pb = p.astype(BF16)
    pv = _dot(pb[:, 0:page], matrix(v_refs[0], v_rows))
    for i in range(1, pages):
        pv = pv + _dot(pb[:, i * page:(i + 1) * page], matrix(v_refs[i], v_rows))
    acc_sc[...] = alpha * acc_sc[...] + pv
    m_sc[...] = m_new

    @pl.when(c == pl.num_programs(1) - 1)
    def _():
        lam = _lambda_value(lam_ref, lam_init)
        o = acc_sc[...] / l_sc[...]
        for h in range(DIFF_KV_HEADS):
            for g in range(DIFF_GROUP):
                r0 = h * 4 + g
                r1 = h * 4 + 2 + g
                cols = slice(h * DIFF_V_DIM, (h + 1) * DIFF_V_DIM)
                res = _diff_finish(o[r0:r0 + 1, cols], o[r1:r1 + 1, cols], lam, subln_ref[...], lam_init)
                o_ref[:, pl.ds((h * DIFF_GROUP + g) * DIFF_V_DIM, DIFF_V_DIM)] = res.astype(o_ref.dtype)


def _diff_sample(u, cache_k, cache_v, page_table, rel_table, lam_params, subln, lam_init, pages=8):
    db = u.shape[0]
    n_pages = page_table.shape[1]
    page = cache_k.shape[1] // 8
    pages = min(pages, n_pages)
    assert n_pages % pages == 0
    past = n_pages * page
    order = np.array([h * 4 + g * 2 + mp for h in range(DIFF_KV_HEADS) for mp in range(2) for g in range(DIFF_GROUP)])
    q = u[:, :DIFF_Q_W].reshape(db, DIFF_KV_HEADS, DIFF_GROUP, 2, DIFF_HEAD_DIM)
    q = q.transpose(0, 1, 3, 2, 4).reshape(db, DIFF_MAPS, DIFF_HEAD_DIM)
    q_t = jnp.tile(q, (1, 1, DIFF_K_W // DIFF_HEAD_DIM))
    k_new = u[:, DIFF_Q_W:DIFF_Q_W + DIFF_K_W].reshape(db, 1, DIFF_K_W)
    v_new = u[:, DIFF_Q_W + DIFF_K_W:].reshape(db, 1, DIFF_V_W)
    table = rel_table[:, order]
    bias = _bias_tiles(table, 8, past, past, 0, ANY_DIST)[:, 0]
    bias0 = table[0].reshape(DIFF_MAPS, 1)
    pt_flat = page_table.reshape(-1)

    def page_spec(i):
        return pl.BlockSpec((None, page * 8, LANES),
                            lambda b, c, pt: (pt[b * n_pages + c * pages + i], 0, 0))

    const2 = lambda b, c, pt: (0, 0)
    grid_spec = pltpu.PrefetchScalarGridSpec(
        num_scalar_prefetch=1,
        grid=(db, n_pages // pages),
        in_specs=[pl.BlockSpec((None, DIFF_MAPS, DIFF_K_W), lambda b, c, pt: (b, 0, 0)),
                  pl.BlockSpec((None, 1, DIFF_K_W), lambda b, c, pt: (b, 0, 0)),
                  pl.BlockSpec((None, 1, DIFF_V_W), lambda b, c, pt: (b, 0, 0)),
                  pl.BlockSpec(bias.shape, const2),
                  pl.BlockSpec((DIFF_MAPS, 1), const2),
                  pl.BlockSpec((4, DIFF_HEAD_DIM), const2),
                  pl.BlockSpec((1, DIFF_V_DIM), const2)]
                 + [page_spec(i) for i in range(pages)] * 2,
        out_specs=pl.BlockSpec((None, 1, DIFF_OUT_W), lambda b, c, pt: (b, 0, 0)),
        scratch_shapes=[pltpu.VMEM((DIFF_MAPS, 1), F32), pltpu.VMEM((DIFF_MAPS, 1), F32),
                        pltpu.VMEM((DIFF_MAPS, DIFF_V_W), F32)],
    )
    o = pl.pallas_call(
        functools.partial(_diff_sample_kernel, pages=pages, lam_init=lam_init),
        out_shape=jax.ShapeDtypeStruct((db, 1, DIFF_OUT_W), BF16),
        grid_spec=grid_spec,
        compiler_params=_params("arbitrary", "arbitrary"),
        name="diff_attn_sample",
    )(pt_flat, q_t, k_new, v_new, bias, bias0, lam_params, subln.reshape(1, DIFF_V_DIM),
      *([cache_k] * pages), *([cache_v] * pages))
    return o.reshape(db, DIFF_OUT_W)


def _routing_tables(idx, tm):
    m = idx.shape[0]
    e_flat = idx.reshape(-1)
    onehot = (e_flat[:, None] == jnp.arange(N_EXPERTS)[None, :]).astype(jnp.int32)
    rank = jnp.sum((jnp.cumsum(onehot, axis=0) - onehot) * onehot, axis=1)
    counts = jnp.sum(onehot, axis=0)
    tiles = (counts + tm - 1) // tm
    tile_end = jnp.cumsum(tiles)
    tile_start = tile_end - tiles
    pos = tile_start[e_flat] * tm + rank
    n_tiles = (TOP_K * m + tm - 1) // tm + N_EXPERTS
    t_ids = jnp.arange(n_tiles)
    tile_valid = (t_ids < tile_end[-1]).astype(jnp.int32)
    last_used = jnp.max(jnp.where(counts > 0, jnp.arange(N_EXPERTS), 0))
    tile_expert = jnp.sum((t_ids[:, None] >= tile_end[None, :]).astype(jnp.int32), axis=1)
    tile_expert = jnp.minimum(tile_expert, last_used).astype(jnp.int32)
    src = jnp.zeros((n_tiles * tm,), jnp.int32).at[pos].set(jnp.arange(TOP_K * m, dtype=jnp.int32) // TOP_K)
    pos = pos.reshape(m, TOP_K).astype(jnp.int32)
    return pos[:, 0], pos[:, 1], src, tile_expert, tile_valid


def kernel(x_prompt, x_sample, mem_prompt, state_pool, state_swa_k, state_swa_v, cache_diff_k, cache_diff_v,
           cache_mem_k, cache_mem_v, page_table, rel_bias_table, norm_gain, final_gain, w_in_even, pool_map,
           pool_scale, attn_sinks, w_out_even, w_qkv_odd, diff_lambda, diff_subln, w_out_odd, w_mem_q, w_mem_k,
           w_mem_v, w_mem_o, w_ffn_gu, w_ffn_down, w_router, w_exp_gu, w_exp_down):
    _, s, d = x_prompt.shape
    db = x_sample.shape[0]
    n_phys, page = cache_diff_k.shape[:2]
    hp = x_prompt.reshape(s, d)
    hs = x_sample.reshape(db, d)
    mem = mem_prompt.reshape(-1, d)
    ml = mem.shape[0]
    bias_swa = _swa_bias(rel_bias_table)
    n_layers = cache_mem_k.shape[0]
    cmk = cache_mem_k.reshape(n_layers, db, -1, MEM_HEAD_DIM)
    cmv = cache_mem_v.reshape(n_layers, db, -1, MEM_HEAD_DIM)

    def mem_block(hp, hs, layer):
        mk = _matmul(mem, w_mem_k[layer])
        mv = _matmul(mem, w_mem_v[layer])
        g_mem = norm_gain[layer, 1]
        qp = _matmul(_rmsnorm(hp, g_mem), w_mem_q[layer])
        hp = _matmul(_mem_prompt(qp, mk, mv), w_mem_o[layer], resid=hp)
        qs = _matmul(_rmsnorm(hs, g_mem), w_mem_q[layer])
        os_ = _mem_sample(qs, cmk, cmv, layer)
        hs = _matmul(os_, w_mem_o[layer], resid=hs)
        return hp, hs, mk, mv

    g_mix = norm_gain[0, 0]
    up = _matmul(_rmsnorm(hp, g_mix), w_in_even[0])
    us = _matmul(_rmsnorm(hs, g_mix), w_in_even[0])
    a_p = _pool_prompt(up, pool_map[0], pool_scale[0])
    b_p = _swa_prompt(up, bias_swa, attn_sinks[0])
    hp = _matmul(jnp.concatenate([a_p, b_p], axis=1), w_out_even[0], resid=hp)

    o1 = POOL_WIDTH
    o2 = o1 + SWA_Q_W
    o3 = o2 + SWA_KV_W
    a_s = _pool_sample(state_pool[0], us, pool_map[0], pool_scale[0])
    k_old = state_swa_k[0].reshape(db, WINDOW, SWA_KV_W)
    v_old = state_swa_v[0].reshape(db, WINDOW, SWA_KV_W)
    k_state = jnp.concatenate([k_old[:, 1:], us[:, None, o2:o3]], axis=1)
    v_state = jnp.concatenate([v_old[:, 1:], us[:, None, o3:]], axis=1)
    b_s = _swa_sample(us[:, o1:o2].reshape(db, SWA_HEADS, SWA_HEAD_DIM), k_state, v_state,
                      k_old[:, :1], v_old[:, :1], bias_swa, attn_sinks[0])
    hs = _matmul(jnp.concatenate([a_s, b_s], axis=1), w_out_even[0], resid=hs)

    pool_p = up[s - POOL_STATE:, :o1].reshape(1, 1, POOL_STATE, POOL_WIDTH)
    pool_s = jnp.concatenate([state_pool[0][:, 1:], us[:, None, :o1]], axis=1)[None]
    swk_p = up[s - WINDOW:, o2:o3].reshape(1, 1, WINDOW, SWA_KV_HEADS, SWA_HEAD_DIM)
    swv_p = up[s - WINDOW:, o3:].reshape(1, 1, WINDOW, SWA_KV_HEADS, SWA_HEAD_DIM)
    swk_s = k_state.reshape(1, db, WINDOW, SWA_KV_HEADS, SWA_HEAD_DIM)
    swv_s = v_state.reshape(1, db, WINDOW, SWA_KV_HEADS, SWA_HEAD_DIM)

    hp, hs, mk0, mv0 = mem_block(hp, hs, 0)

    g_ffn = norm_gain[0, 2]
    hp = _ffn(_rmsnorm(hp, g_ffn), w_ffn_gu[0], w_ffn_down[0], hp)
    hs = _ffn(_rmsnorm(hs, g_ffn), w_ffn_gu[0], w_ffn_down[0], hs)

    lam_init = 0.8 - 0.6 * math.exp(-0.3 * 1)
    g_mix = norm_gain[1, 0]
    up = _matmul(_rmsnorm(hp, g_mix), w_qkv_odd[0])
    us = _matmul(_rmsnorm(hs, g_mix), w_qkv_odd[0])
    o_p = _diff_prompt(up, rel_bias_table, diff_lambda[0], diff_subln[0], lam_init)
    hp = _matmul(o_p, w_out_odd[0], resid=hp)
    ck = cache_diff_k.reshape(n_phys, page * 8, LANES)
    cv = cache_diff_v.reshape(n_phys, page, DIFF_KV_HEADS, 2, LANES).transpose(0, 1, 3, 2, 4)
    cv = cv.reshape(n_phys, page * 8, LANES)
    o_s = _diff_sample(us, ck, cv, page_table, rel_bias_table, diff_lambda[0], diff_subln[0], lam_init)
    hs = _matmul(o_s, w_out_odd[0], resid=hs)

    q_end = DIFF_Q_W
    k_end = q_end + DIFF_K_W
    dk_p = up[:, q_end:k_end].reshape(1, s, 1, DIFF_KV_HEADS, 2, DIFF_HEAD_DIM)
    dv_p = up[:, k_end:].reshape(1, s, 1, DIFF_KV_HEADS, DIFF_V_DIM)
    dk_s = us[:, q_end:k_end].reshape(db, 1, 1, DIFF_KV_HEADS, 2, DIFF_HEAD_DIM)
    dv_s = us[:, k_end:].reshape(db, 1, 1, DIFF_KV_HEADS, DIFF_V_DIM)

    hp, hs, mk1, mv1 = mem_block(hp, hs, 1)

    h_all = jnp.concatenate([hp, hs], axis=0)
    x_all = _rmsnorm(h_all, norm_gain[1, 2], out_dtype=F32)
    idx, gates = _router(x_all, w_router[0])
    tm = 768
    pos1, pos2, src, tile_expert, tile_valid = _routing_tables(idx[:, :TOP_K], tm)
    x_sorted = _gather_rows(x_all, src)
    y_sorted = _moe_experts(x_sorted, tile_expert, tile_valid, w_exp_gu[0], w_exp_down[0], tm)
    y_all = _moe_combine_norm(y_sorted, pos1, pos2, gates, h_all, final_gain)

    mem_shape = (1, ml, MEM_HEADS, MEM_HEAD_DIM)
    return (y_all[:s].reshape(1, s, d), y_all[s:].reshape(db, 1, d),
            pool_p, pool_s, swk_p, swv_p, swk_s, swv_s,
            dk_p, dv_p, dk_s, dv_s,
            jnp.stack([mk0.reshape(mem_shape), mk1.reshape(mem_shape)]),
            jnp.stack([mv0.reshape(mem_shape), mv1.reshape(mem_shape)]))
```

```python
import functools
import math

import numpy as np
import jax
import jax.numpy as jnp
from jax import lax
from jax.experimental import pallas as pl
from jax.experimental.pallas import tpu as pltpu

F32 = jnp.float32
BF16 = jnp.bfloat16
EPS = 1e-6

REL_BUCKETS = 32
REL_MAX_EXACT = 16
REL_MAX_DIST = 128
REL_HEADS = 16
POOL_WINDOWS = (2, 4, 8, 16)
POOL_WIDTH = 1024
POOL_GROUP = POOL_WIDTH // len(POOL_WINDOWS)
POOL_STATE = max(POOL_WINDOWS) - 1
WINDOW = 128
SWA_HEADS = 16
SWA_KV_HEADS = 4
SWA_HEAD_DIM = 64
SWA_GROUP = SWA_HEADS // SWA_KV_HEADS
SWA_Q_W = SWA_HEADS * SWA_HEAD_DIM
SWA_KV_W = SWA_KV_HEADS * SWA_HEAD_DIM
DIFF_KV_HEADS = 4
DIFF_GROUP = 2
DIFF_HEAD_DIM = 128
DIFF_V_DIM = 256
DIFF_MAPS = DIFF_KV_HEADS * DIFF_GROUP * 2
DIFF_Q_W = DIFF_MAPS * DIFF_HEAD_DIM
DIFF_K_W = DIFF_KV_HEADS * 2 * DIFF_HEAD_DIM
DIFF_V_W = DIFF_KV_HEADS * DIFF_V_DIM
DIFF_OUT_W = DIFF_KV_HEADS * DIFF_GROUP * DIFF_V_DIM
MEM_HEADS = 4
MEM_HEAD_DIM = 128
MEM_W = MEM_HEADS * MEM_HEAD_DIM
N_EXPERTS = 8
TOP_K = 2

ANY_DIST = 2 ** 30
LANES = 128
VMEM_LIMIT = 56 * 1024 * 1024


def _params(*sem):
    return pltpu.CompilerParams(dimension_semantics=sem, vmem_limit_bytes=VMEM_LIMIT)


def _tile(n, pref):
    if n <= pref:
        return n
    t = pref
    while n % t:
        t -= 8
    return t


def _dot(a, b):
    return jnp.dot(a, b, preferred_element_type=F32)


def _dot_nt(a, b):
    return lax.dot_general(a, b, (((1,), (1,)), ((), ())), preferred_element_type=F32)


def _rmsnorm_kernel(x_ref, g_ref, o_ref):
    x = x_ref[...]
    y = x * lax.rsqrt(jnp.mean(x * x, axis=-1, keepdims=True) + EPS)
    o_ref[...] = (y * g_ref[...]).astype(o_ref.dtype)


def _rmsnorm(x, gain, out_dtype=BF16):
    m, d = x.shape
    tr = _tile(m, 512)
    return pl.pallas_call(
        _rmsnorm_kernel,
        out_shape=jax.ShapeDtypeStruct((m, d), out_dtype),
        grid=(m // tr,),
        in_specs=[pl.BlockSpec((tr, d), lambda i: (i, 0)),
                  pl.BlockSpec((1, d), lambda i: (0, 0))],
        out_specs=pl.BlockSpec((tr, d), lambda i: (i, 0)),
        compiler_params=_params("parallel"),
        name="rmsnorm",
    )(x, gain.reshape(1, d))


def _mm_kernel(*refs, has_resid):
    if has_resid:
        a_ref, w_ref, r_ref, o_ref = refs
    else:
        a_ref, w_ref, o_ref = refs
    acc = _dot(a_ref[...].astype(BF16), w_ref[...].astype(BF16))
    if has_resid:
        acc = r_ref[...] + acc
    o_ref[...] = acc.astype(o_ref.dtype)


def _matmul(a, w, resid=None, tm_pref=1024, tn_pref=512):
    m, k = a.shape
    n = w.shape[1]
    tm = _tile(m, tm_pref)
    tn = _tile(n, tn_pref)
    in_specs = [pl.BlockSpec((tm, k), lambda i, j: (i, 0)),
                pl.BlockSpec((k, tn), lambda i, j: (0, j))]
    args = [a, w]
    if resid is not None:
        in_specs.append(pl.BlockSpec((tm, tn), lambda i, j: (i, j)))
        args.append(resid)
    return pl.pallas_call(
        functools.partial(_mm_kernel, has_resid=resid is not None),
        out_shape=jax.ShapeDtypeStruct((m, n), F32),
        grid=(m // tm, n // tn),
        in_specs=in_specs,
        out_specs=pl.BlockSpec((tm, tn), lambda i, j: (i, j)),
        compiler_params=_params("parallel", "parallel"),
        name="matmul",
    )(*args)


def _swiglu_step(x, wg_ref, wu_ref, wd_ref):
    g = _dot(x, wg_ref[...].astype(BF16))
    u = _dot(x, wu_ref[...].astype(BF16))
    h = (g * jax.nn.sigmoid(g) * u).astype(BF16)
    return _dot(h, wd_ref[...].astype(BF16))


def _ffn_kernel(x_ref, wg_ref, wu_ref, wd_ref, r_ref, o_ref):
    @pl.when(pl.program_id(1) == 0)
    def _():
        o_ref[...] = r_ref[...]

    o_ref[...] += _swiglu_step(x_ref[...], wg_ref, wu_ref, wd_ref)


def _ffn(x, w_gu, w_down, resid, tm_pref=1024, tf_pref=256):
    m, d = x.shape
    f = w_down.shape[0]
    tm = _tile(m, tm_pref)
    tf = _tile(f, tf_pref)
    nf = f // tf
    return pl.pallas_call(
        _ffn_kernel,
        out_shape=jax.ShapeDtypeStruct((m, d), F32),
        grid=(m // tm, nf),
        in_specs=[pl.BlockSpec((tm, d), lambda i, j: (i, 0)),
                  pl.BlockSpec((d, tf), lambda i, j: (0, j)),
                  pl.BlockSpec((d, tf), lambda i, j: (0, j + nf)),
                  pl.BlockSpec((tf, d), lambda i, j: (j, 0)),
                  pl.BlockSpec((tm, d), lambda i, j: (i, 0))],
        out_specs=pl.BlockSpec((tm, d), lambda i, j: (i, 0)),
        compiler_params=_params("parallel", "arbitrary"),
        name="ffn_swiglu",
    )(x, w_gu, w_gu, w_down, resid)


def _moe_kernel(te_ref, tv_ref, x_ref, wg_ref, wu_ref, wd_ref, o_ref):
    i = pl.program_id(0)

    @pl.when(tv_ref[i] > 0)
    def _():
        y = _swiglu_step(x_ref[...].astype(BF16), wg_ref, wu_ref, wd_ref)

        @pl.when(pl.program_id(1) == 0)
        def _():
            o_ref[...] = y

        @pl.when(pl.program_id(1) > 0)
        def _():
            o_ref[...] += y


def _moe_experts(x_sorted, tile_expert, tile_valid, w_gu, w_down, tm, tf_pref=512):
    np_rows, d = x_sorted.shape
    f = w_down.shape[1]
    tf = _tile(f, tf_pref)
    nf = f // tf

    def chunk(j, tv, i):
        return jnp.where(tv[i] > 0, j, nf - 1)

    grid_spec = pltpu.PrefetchScalarGridSpec(
        num_scalar_prefetch=2,
        grid=(np_rows // tm, nf),
        in_specs=[pl.BlockSpec((tm, d), lambda i, j, te, tv: (i, 0)),
                  pl.BlockSpec((None, d, tf), lambda i, j, te, tv: (te[i], 0, chunk(j, tv, i))),
                  pl.BlockSpec((None, d, tf), lambda i, j, te, tv: (te[i], 0, chunk(j, tv, i) + nf)),
                  pl.BlockSpec((None, tf, d), lambda i, j, te, tv: (te[i], chunk(j, tv, i), 0))],
        out_specs=pl.BlockSpec((tm, d), lambda i, j, te, tv: (i, 0)),
    )
    return pl.pallas_call(
        _moe_kernel,
        out_shape=jax.ShapeDtypeStruct((np_rows, d), F32),
        grid_spec=grid_spec,
        compiler_params=_params("arbitrary", "arbitrary"),
        name="moe_experts",
    )(tile_expert, tile_valid, x_sorted, w_gu, w_gu, w_down)


def _router_kernel(x_ref, w_ref, idx_ref, gate_ref):
    lg = _dot(x_ref[...].astype(BF16), w_ref[...].astype(BF16))
    col = lax.broadcasted_iota(jnp.int32, lg.shape, 1)
    lg = jnp.where(col < N_EXPERTS, lg, -jnp.inf)
    v1 = jnp.max(lg, axis=-1, keepdims=True)
    i1 = jnp.min(jnp.where(lg == v1, col, LANES), axis=-1, keepdims=True)
    lg2 = jnp.where(col == i1, -jnp.inf, lg)
    v2 = jnp.max(lg2, axis=-1, keepdims=True)
    i2 = jnp.min(jnp.where(lg2 == v2, col, LANES), axis=-1, keepdims=True)
    e2 = jnp.exp(v2 - v1)
    den = 1.0 + e2
    idx_ref[...] = jnp.where(col == 0, i1, jnp.where(col == 1, i2, 0))
    gate_ref[...] = jnp.where(col == 0, 1.0 / den, jnp.where(col == 1, e2 / den, 0.0))


def _router(x, w_router):
    m, d = x.shape
    w_pad = jnp.zeros((d, LANES), F32).at[:, :N_EXPERTS].set(w_router)
    tm = _tile(m, 640)
    return pl.pallas_call(
        _router_kernel,
        out_shape=(jax.ShapeDtypeStruct((m, LANES), jnp.int32),
                   jax.ShapeDtypeStruct((m, LANES), F32)),
        grid=(m // tm,),
        in_specs=[pl.BlockSpec((tm, d), lambda i: (i, 0)),
                  pl.BlockSpec((d, LANES), lambda i: (0, 0))],
        out_specs=(pl.BlockSpec((tm, LANES), lambda i: (i, 0)),
                   pl.BlockSpec((tm, LANES), lambda i: (i, 0))),
        compiler_params=_params("parallel"),
        name="router_top2",
    )(x, w_pad)


def _row_copy(src_hbm, row, dst_ref, r, sem):
    return pltpu.make_async_copy(src_hbm.at[pl.ds(row, 1)], dst_ref.at[pl.ds(r, 1)], sem)


def _gather_kernel(src_ref, x_hbm, o_ref, sem, *, tg):
    base = pl.program_id(0) * tg

    def issue(r, c):
        _row_copy(x_hbm, src_ref[base + r], o_ref, r, sem).start()
        return c

    lax.fori_loop(0, tg, issue, 0, unroll=8)

    def drain(r, c):
        _row_copy(x_hbm, 0, o_ref, r, sem).wait()
        return c

    lax.fori_loop(0, tg, drain, 0, unroll=8)


def _gather_rows(x, src, tg=256):
    n = src.shape[0]
    d = x.shape[1]
    grid_spec = pltpu.PrefetchScalarGridSpec(
        num_scalar_prefetch=1,
        grid=(n // tg,),
        in_specs=[pl.BlockSpec(memory_space=pl.ANY)],
        out_specs=pl.BlockSpec((tg, d), lambda i, s: (i, 0)),
        scratch_shapes=[pltpu.SemaphoreType.DMA(())],
    )
    return pl.pallas_call(
        functools.partial(_gather_kernel, tg=tg),
        out_shape=jax.ShapeDtypeStruct((n, d), x.dtype),
        grid_spec=grid_spec,
        compiler_params=_params("arbitrary"),
        name="gather_rows",
    )(src, x)


def _combine_kernel(p1_ref, p2_ref, y_hbm, gate_ref, r_ref, fg_ref, o_ref, buf, sem, *, tc):
    base = pl.program_id(0) * tc

    def issue(r, c):
        _row_copy(y_hbm, p1_ref[base + r], buf.at[0], r, sem).start()
        _row_copy(y_hbm, p2_ref[base + r], buf.at[1], r, sem).start()
        return c

    lax.fori_loop(0, tc, issue, 0, unroll=8)

    def drain(r, c):
        _row_copy(y_hbm, 0, buf.at[0], r, sem).wait()
        _row_copy(y_hbm, 0, buf.at[1], r, sem).wait()
        return c

    lax.fori_loop(0, tc, drain, 0, unroll=8)

    g = gate_ref[...]
    h = r_ref[...] + (buf[0] * g[:, 0:1] + buf[1] * g[:, 1:2])
    y = h * lax.rsqrt(jnp.mean(h * h, axis=-1, keepdims=True) + EPS)
    o_ref[...] = y * fg_ref[...]


def _moe_combine_norm(y_sorted, pos1, pos2, gates, resid, final_gain, tc=128):
    m, d = resid.shape
    grid_spec = pltpu.PrefetchScalarGridSpec(
        num_scalar_prefetch=2,
        grid=(m // tc,),
        in_specs=[pl.BlockSpec(memory_space=pl.ANY),
                  pl.BlockSpec((tc, LANES), lambda i, a, b: (i, 0)),
                  pl.BlockSpec((tc, d), lambda i, a, b: (i, 0)),
                  pl.BlockSpec((1, d), lambda i, a, b: (0, 0))],
        out_specs=pl.BlockSpec((tc, d), lambda i, a, b: (i, 0)),
        scratch_shapes=[pltpu.VMEM((2, tc, d), F32), pltpu.SemaphoreType.DMA(())],
    )
    return pl.pallas_call(
        functools.partial(_combine_kernel, tc=tc),
        out_shape=jax.ShapeDtypeStruct((m, d), F32),
        grid_spec=grid_spec,
        compiler_params=_params("arbitrary"),
        name="moe_combine_norm",
    )(pos1, pos2, y_sorted, gates, resid, final_gain.reshape(1, d))


def _bucket_thresholds():
    n = np.arange(REL_MAX_EXACT, 4 * REL_MAX_DIST, dtype=np.int64)
    ratio = np.log(n.astype(np.float32) / np.float32(REL_MAX_EXACT))
    large = REL_MAX_EXACT + (ratio / np.float32(math.log(REL_MAX_DIST / REL_MAX_EXACT))
                             * np.float32(REL_BUCKETS - REL_MAX_EXACT)).astype(np.int32)
    large = np.minimum(large, REL_BUCKETS - 1)
    return [int(n[np.argmax(large >= b)]) for b in range(REL_MAX_EXACT + 1, REL_BUCKETS)]


def _bias_kernel(t_ref, o_ref, *, off, lo, hi, thresholds):
    h = pl.program_id(0)
    rc, cols = o_ref.shape
    i = pl.program_id(1) * rc + lax.broadcasted_iota(jnp.int32, (rc, cols), 0)
    j = lax.broadcasted_iota(jnp.int32, (rc, cols), 1)
    dist = off + i - j
    n = jnp.maximum(dist, 0)
    bucket = jnp.minimum(n, REL_MAX_EXACT)
    for t in thresholds:
        bucket = bucket + (n >= t).astype(jnp.int32)
    acc = jnp.full((rc, cols), t_ref[REL_BUCKETS - 1, h], F32)
    for b in range(REL_BUCKETS - 1):
        acc = jnp.where(bucket == b, t_ref[b, h], acc)
    o_ref[...] = jnp.where((dist >= lo) & (dist <= hi), acc, -jnp.inf)


def _bias_tiles(rel_table, rows, cols, off, lo, hi):
    rc = _tile(rows, 64)
    return pl.pallas_call(
        functools.partial(_bias_kernel, off=off, lo=lo, hi=hi, thresholds=_bucket_thresholds()),
        out_shape=jax.ShapeDtypeStruct((REL_HEADS, rows, cols), F32),
        grid=(REL_HEADS, rows // rc),
        in_specs=[pl.BlockSpec(memory_space=pltpu.SMEM)],
        out_specs=pl.BlockSpec((None, rc, cols), lambda h, r: (h, r, 0)),
        compiler_params=_params("parallel", "parallel"),
        name="rel_bias_tiles",
    )(rel_table)


def _pool_project(pooled, map_ref, scale_ref, o_ref):
    for g in range(len(POOL_WINDOWS)):
        cols = slice(g * POOL_GROUP, (g + 1) * POOL_GROUP)
        mixed = _dot(pooled[g].astype(BF16), map_ref[g].astype(BF16))
        o_ref[:, cols] = (mixed * scale_ref[:, cols]).astype(o_ref.dtype)


def _pool_prompt_kernel(u_ref, halo_ref, map_ref, scale_ref, o_ref, *, tp):
    i = pl.program_id(0)
    halo_rows = halo_ref.shape[0]
    halo = jnp.where(i > 0, halo_ref[...], 0.0)
    ext = jnp.concatenate([halo, u_ref[...]], axis=0)
    row = i * tp + lax.broadcasted_iota(jnp.int32, (tp, 1), 0)
    pooled = []
    for g, w in enumerate(POOL_WINDOWS):
        cols = slice(g * POOL_GROUP, (g + 1) * POOL_GROUP)
        s = ext[:, cols]
        shift = 1
        while shift < w:
            s = s + pltpu.roll(s, shift, axis=0)
            shift *= 2
        cnt = jnp.minimum(row + 1, w).astype(F32)
        x = ext[halo_rows:, cols]
        pooled.append(s[halo_rows:] / cnt - x)
    _pool_project(pooled, map_ref, scale_ref, o_ref)


def _pool_prompt(u, pool_map, pool_scale, tp=256):
    s = u.shape[0]
    tp = _tile(s, tp)
    halo = 16
    return pl.pallas_call(
        functools.partial(_pool_prompt_kernel, tp=tp),
        out_shape=jax.ShapeDtypeStruct((s, POOL_WIDTH), BF16),
        grid=(s // tp,),
        in_specs=[pl.BlockSpec((tp, POOL_WIDTH), lambda i: (i, 0)),
                  pl.BlockSpec((halo, POOL_WIDTH), lambda i: (jnp.maximum(i * (tp // halo) - 1, 0), 0)),
                  pl.BlockSpec(pool_map.shape, lambda i: (0, 0, 0)),
                  pl.BlockSpec((1, POOL_WIDTH), lambda i: (0, 0))],
        out_specs=pl.BlockSpec((tp, POOL_WIDTH), lambda i: (i, 0)),
        compiler_params=_params("parallel"),
        name="pool_prompt",
    )(u, u, pool_map, pool_scale.reshape(1, POOL_WIDTH))


def _pool_sample_kernel(st_ref, u_ref, map_ref, scale_ref, o_ref):
    pooled = []
    for g, w in enumerate(POOL_WINDOWS):
        cols = slice(g * POOL_GROUP, (g + 1) * POOL_GROUP)
        x = u_ref[:, cols]
        s = x
        for k in range(1, w):
            s = s + st_ref[:, POOL_STATE - k, cols]
        pooled.append(s / float(w) - x)
    _pool_project(pooled, map_ref, scale_ref, o_ref)


def _pool_sample(state, u, pool_map, pool_scale):
    db = state.shape[0]
    return pl.pallas_call(
        _pool_sample_kernel,
        out_shape=jax.ShapeDtypeStruct((db, POOL_WIDTH), BF16),
        grid=(1,),
        in_specs=[pl.BlockSpec(state.shape, lambda i: (0, 0, 0)),
                  pl.BlockSpec((db, POOL_WIDTH), lambda i: (0, 0)),
                  pl.BlockSpec(pool_map.shape, lambda i: (0, 0, 0)),
                  pl.BlockSpec((1, POOL_WIDTH), lambda i: (0, 0))],
        out_specs=pl.BlockSpec((db, POOL_WIDTH), lambda i: (0, 0)),
        compiler_params=_params("arbitrary"),
        name="pool_sample",
    )(state, u, pool_map, pool_scale.reshape(1, POOL_WIDTH))


def _swa_prompt_kernel(q_ref, kp_ref, kc_ref, vp_ref, vc_ref, bias_ref, sink_ref, o_ref):
    i = pl.program_id(0)
    col = lax.broadcasted_iota(jnp.int32, (WINDOW, 2 * WINDOW), 1)
    key_ok = (i > 0) | (col >= WINDOW)
    k_all = jnp.concatenate([kp_ref[...], kc_ref[...]], axis=0).astype(BF16)
    v_all = jnp.concatenate([vp_ref[...], vc_ref[...]], axis=0).astype(BF16)
    scale = SWA_HEAD_DIM ** -0.5
    for h in range(SWA_KV_HEADS):
        kv_cols = slice(h * SWA_HEAD_DIM, (h + 1) * SWA_HEAD_DIM)
        k_h = k_all[:, kv_cols]
        v_h = v_all[:, kv_cols]
        for g in range(SWA_GROUP):
            head = h * SWA_GROUP + g
            q_cols = slice(head * SWA_HEAD_DIM, (head + 1) * SWA_HEAD_DIM)
            s = _dot_nt(q_ref[:, q_cols].astype(BF16), k_h) * scale + bias_ref[head]
            s = jnp.where(key_ok, s, -jnp.inf)
            sink = sink_ref[head]
            m = jnp.maximum(jnp.max(s, axis=-1, keepdims=True), sink)
            p = jnp.exp(s - m)
            p = p / (jnp.sum(p, axis=-1, keepdims=True) + jnp.exp(sink - m))
            o_ref[:, q_cols] = _dot(p.astype(BF16), v_h).astype(o_ref.dtype)


def _swa_bias(rel_table):
    return _bias_tiles(rel_table, WINDOW, 2 * WINDOW, WINDOW, 0, WINDOW)


def _swa_prompt(u, bias, sinks):
    s = u.shape[0]
    nb = s // WINDOW
    qb = POOL_WIDTH // SWA_Q_W
    kb = (POOL_WIDTH + SWA_Q_W) // SWA_KV_W
    prev = lambda i: (jnp.maximum(i - 1, 0), kb)
    return pl.pallas_call(
        _swa_prompt_kernel,
        out_shape=jax.ShapeDtypeStruct((s, SWA_Q_W), BF16),
        grid=(nb,),
        in_specs=[pl.BlockSpec((WINDOW, SWA_Q_W), lambda i: (i, qb)),
                  pl.BlockSpec((WINDOW, SWA_KV_W), prev),
                  pl.BlockSpec((WINDOW, SWA_KV_W), lambda i: (i, kb)),
                  pl.BlockSpec((WINDOW, SWA_KV_W), lambda i: (jnp.maximum(i - 1, 0), kb + 1)),
                  pl.BlockSpec((WINDOW, SWA_KV_W), lambda i: (i, kb + 1)),
                  pl.BlockSpec(bias.shape, lambda i: (0, 0, 0)),
                  pl.BlockSpec(memory_space=pltpu.SMEM)],
        out_specs=pl.BlockSpec((WINDOW, SWA_Q_W), lambda i: (i, 0)),
        compiler_params=_params("parallel"),
        name="swa_prompt",
    )(u, u, u, u, u, bias, sinks)


def _swa_sample_kernel(q_ref, k_ref, v_ref, k0_ref, v0_ref, bias_ref, bias0_ref, sink_ref, o_ref):
    scale = SWA_HEAD_DIM ** -0.5
    lane = lax.broadcasted_iota(jnp.int32, (1, SWA_HEADS, SWA_KV_W), 2)
    row = lax.broadcasted_iota(jnp.int32, (1, SWA_HEADS, SWA_KV_W), 1)
    own = (lane // SWA_HEAD_DIM) == (row // SWA_GROUP)
    q_bd = jnp.where(own, q_ref[...], 0.0).astype(BF16)
    k = k_ref[...].astype(BF16)
    v = v_ref[...].astype(BF16)
    s = jnp.einsum('bhc,bkc->bhk', q_bd, k, preferred_element_type=F32) * scale + bias_ref[...][None]
    k0 = k0_ref[...].astype(BF16).astype(F32)
    s0 = jnp.sum(q_bd.astype(F32) * k0, axis=-1, keepdims=True) * scale + bias0_ref[...][None]
    sink = sink_ref[...][None]
    m = jnp.maximum(jnp.maximum(jnp.max(s, axis=-1, keepdims=True), s0), sink)
    p = jnp.exp(s - m)
    p0 = jnp.exp(s0 - m)
    den = jnp.sum(p, axis=-1, keepdims=True) + p0 + jnp.exp(sink - m)
    o = jnp.einsum('bhk,bkc->bhc', (p / den).astype(BF16), v, preferred_element_type=F32)
    o = o + (p0 / den).astype(BF16).astype(F32) * v0_ref[...].astype(BF16).astype(F32)
    for h in range(SWA_KV_HEADS):
        rows = slice(h * SWA_GROUP, (h + 1) * SWA_GROUP)
        o_ref[:, rows, :] = o[:, rows, h * SWA_HEAD_DIM:(h + 1) * SWA_HEAD_DIM].astype(o_ref.dtype)


def _swa_sample(q, k_state, v_state, k0, v0, bias_swa, sinks, sb=8):
    db = q.shape[0]
    sb = _tile(db, sb)
    q_t = jnp.tile(q, (1, 1, SWA_KV_HEADS))
    bias = bias_swa[:, 0, 1:WINDOW + 1]
    bias0 = bias_swa[:, 0, 0:1]
    blk3 = lambda a, b: pl.BlockSpec((sb, a, b), lambda i: (i, 0, 0))
    o = pl.pallas_call(
        _swa_sample_kernel,
        out_shape=jax.ShapeDtypeStruct((db, SWA_HEADS, SWA_HEAD_DIM), BF16),
        grid=(db // sb,),
        in_specs=[blk3(SWA_HEADS, SWA_KV_W), blk3(WINDOW, SWA_KV_W), blk3(WINDOW, SWA_KV_W),
                  blk3(1, SWA_KV_W), blk3(1, SWA_KV_W),
                  pl.BlockSpec((SWA_HEADS, WINDOW), lambda i: (0, 0)),
                  pl.BlockSpec((SWA_HEADS, 1), lambda i: (0, 0)),
                  pl.BlockSpec((SWA_HEADS, 1), lambda i: (0, 0))],
        out_specs=blk3(SWA_HEADS, SWA_HEAD_DIM),
        compiler_params=_params("parallel"),
        name="swa_sample",
    )(q_t, k_state, v_state, k0, v0, bias, bias0, sinks.reshape(SWA_HEADS, 1))
    return o.reshape(db, SWA_Q_W)


def _mem_prompt_kernel(q_ref, k_ref, v_ref, o_ref):
    scale = MEM_HEAD_DIM ** -0.5
    for h in range(MEM_HEADS):
        cols = slice(h * MEM_HEAD_DIM, (h + 1) * MEM_HEAD_DIM)
        s = _dot_nt(q_ref[:, cols].astype(BF16), k_ref[:, cols].astype(BF16)) * scale
        p = jnp.exp(s - jnp.max(s, axis=-1, keepdims=True))
        p = p / jnp.sum(p, axis=-1, keepdims=True)
        o_ref[:, cols] = _dot(p.astype(BF16), v_ref[:, cols].astype(BF16)).astype(o_ref.dtype)


def _mem_prompt(q, mk, mv, tq=512):
    s = q.shape[0]
    tq = _tile(s, tq)
    return pl.pallas_call(
        _mem_prompt_kernel,
        out_shape=jax.ShapeDtypeStruct((s, MEM_W), BF16),
        grid=(s // tq,),
        in_specs=[pl.BlockSpec((tq, MEM_W), lambda i: (i, 0)),
                  pl.BlockSpec(mk.shape, lambda i: (0, 0)),
                  pl.BlockSpec(mv.shape, lambda i: (0, 0))],
        out_specs=pl.BlockSpec((tq, MEM_W), lambda i: (i, 0)),
        compiler_params=_params("parallel"),
        name="mem_attn_prompt",
    )(q, mk, mv)


def _mem_sample_kernel(q_ref, k_ref, v_ref, o_ref):
    scale = MEM_HEAD_DIM ** -0.5
    lane = lax.broadcasted_iota(jnp.int32, (1, MEM_HEADS, MEM_W), 2)
    row = lax.broadcasted_iota(jnp.int32, (1, MEM_HEADS, MEM_W), 1)
    own = (lane // MEM_HEAD_DIM) == row
    q_bd = jnp.where(own, q_ref[...], 0.0).astype(BF16)
    ml = k_ref.shape[1] // MEM_HEADS

    def matrix(ref):
        return jnp.concatenate([ref[:, pl.ds(hh, ml, stride=MEM_HEADS), :] for hh in range(MEM_HEADS)],
                               axis=2).astype(BF16)

    s = jnp.einsum('bhc,bkc->bhk', q_bd, matrix(k_ref), preferred_element_type=F32) * scale
    p = jnp.exp(s - jnp.max(s, axis=-1, keepdims=True))
    p = p / jnp.sum(p, axis=-1, keepdims=True)
    o = jnp.einsum('bhk,bkc->bhc', p.astype(BF16), matrix(v_ref), preferred_element_type=F32)
    o_ref[...] = jnp.sum(jnp.where(own, o, 0.0), axis=1, keepdims=True).astype(o_ref.dtype)


def _mem_sample(q, mem_k, mem_v, layer, sb=8):
    _, db, rows, _ = mem_k.shape
    sb = _tile(db, sb)
    kv_spec = pl.BlockSpec((None, sb, rows, MEM_HEAD_DIM), lambda i: (layer, i, 0, 0))
    o = pl.pallas_call(
        _mem_sample_kernel,
        out_shape=jax.ShapeDtypeStruct((db, 1, MEM_W), BF16),
        grid=(db // sb,),
        in_specs=[pl.BlockSpec((sb, 1, MEM_W), lambda i: (i, 0, 0)), kv_spec, kv_spec],
        out_specs=pl.BlockSpec((sb, 1, MEM_W), lambda i: (i, 0, 0)),
        compiler_params=_params("parallel"),
        name="mem_attn_sample",
    )(q.reshape(db, 1, MEM_W), mem_k, mem_v)
    return o.reshape(db, MEM_W)


def _lambda_value(lam_ref, lam_init):
    lp = lam_ref[...]
    a = jnp.sum(lp[0:1] * lp[1:2], axis=-1, keepdims=True)
    b = jnp.sum(lp[2:3] * lp[3:4], axis=-1, keepdims=True)
    return jnp.exp(a) - jnp.exp(b) + lam_init


def _diff_finish(o0, o1, lam, subln, lam_init):
    o = o0 - lam * o1
    o = o * lax.rsqrt(jnp.mean(o * o, axis=-1, keepdims=True) + EPS)
    return o * subln * (1.0 - lam_init)


def _diff_prompt_kernel(qi_ref, kj_ref, q_ref, k_ref, v_ref, bias_ref, far_ref, lam_ref, subln_ref,
                        o_ref, m_sc, l_sc, acc_sc, *, lam_init):
    h = pl.program_id(0)
    step = pl.program_id(1)
    qi = qi_ref[step]
    kj = kj_ref[step]
    scale = DIFF_HEAD_DIM ** -0.5
    combos = [(g, mp) for g in range(DIFF_GROUP) for mp in range(2)]

    @pl.when(kj == 0)
    def _():
        m_sc[...] = jnp.full(m_sc.shape, -jnp.inf, F32)
        l_sc[...] = jnp.zeros(l_sc.shape, F32)
        acc_sc[...] = jnp.zeros(acc_sc.shape, F32)

    def update(bias_of):
        v = v_ref[...].astype(BF16)
        for c, (g, mp) in enumerate(combos):
            q = q_ref[:, pl.ds((g * 2 + mp) * DIFF_HEAD_DIM, DIFF_HEAD_DIM)].astype(BF16)
            k = k_ref[:, pl.ds(mp * DIFF_HEAD_DIM, DIFF_HEAD_DIM)].astype(BF16)
            s = _dot_nt(q, k) * scale + bias_of(c)
            m_old = m_sc[c]
            m_new = jnp.maximum(m_old, jnp.max(s, axis=-1, keepdims=True))
            p = jnp.exp(s - m_new)
            alpha = jnp.exp(m_old - m_new)
            l_sc[c] = alpha * l_sc[c] + jnp.sum(p, axis=-1, keepdims=True)
            acc_sc[c] = alpha * acc_sc[c] + _dot(p.astype(BF16), v)
            m_sc[c] = m_new

    @pl.when(kj < qi - 1)
    def _():
        update(lambda c: far_ref[h * 4 + c])

    @pl.when(kj == qi - 1)
    def _():
        update(lambda c: bias_ref[1, c])

    @pl.when(kj == qi)
    def _():
        update(lambda c: bias_ref[0, c])
        lam = _lambda_value(lam_ref, lam_init)
        for g in range(DIFF_GROUP):
            o0 = acc_sc[2 * g] / l_sc[2 * g]
            o1 = acc_sc[2 * g + 1] / l_sc[2 * g + 1]
            o_ref[:, pl.ds(g * DIFF_V_DIM, DIFF_V_DIM)] = _diff_finish(
                o0, o1, lam, subln_ref[...], lam_init).astype(o_ref.dtype)


def _diff_prompt(u, rel_table, lam_params, subln, lam_init, t_pref=512):
    s = u.shape[0]
    t = _tile(s, t_pref)
    nq = s // t
    pairs = [(qi, kj) for qi in range(nq) for kj in range(qi + 1)]
    qi_arr = jnp.asarray([p[0] for p in pairs], jnp.int32)
    kj_arr = jnp.asarray([p[1] for p in pairs], jnp.int32)
    diag = _bias_tiles(rel_table, t, t, 0, 0, ANY_DIST)
    sub = _bias_tiles(rel_table, t, t, t, 0, ANY_DIST)
    bias = jnp.stack([diag, sub]).reshape(2, DIFF_KV_HEADS, 4, t, t)
    far = rel_table[REL_BUCKETS - 1]
    kb = DIFF_Q_W // (2 * DIFF_HEAD_DIM)
    vb = (DIFF_Q_W + DIFF_K_W) // DIFF_V_DIM
    grid_spec = pltpu.PrefetchScalarGridSpec(
        num_scalar_prefetch=2,
        grid=(DIFF_KV_HEADS, len(pairs)),
        in_specs=[pl.BlockSpec((t, 4 * DIFF_HEAD_DIM), lambda h, st, qi, kj: (qi[st], h)),
                  pl.BlockSpec((t, 2 * DIFF_HEAD_DIM), lambda h, st, qi, kj: (kj[st], kb + h)),
                  pl.BlockSpec((t, DIFF_V_DIM), lambda h, st, qi, kj: (kj[st], vb + h)),
                  pl.BlockSpec((2, None, 4, t, t), lambda h, st, qi, kj: (0, h, 0, 0, 0)),
                  pl.BlockSpec(memory_space=pltpu.SMEM),
                  pl.BlockSpec((4, DIFF_HEAD_DIM), lambda h, st, qi, kj: (0, 0)),
                  pl.BlockSpec((1, DIFF_V_DIM), lambda h, st, qi, kj: (0, 0))],
        out_specs=pl.BlockSpec((t, DIFF_GROUP * DIFF_V_DIM), lambda h, st, qi, kj: (qi[st], h)),
        scratch_shapes=[pltpu.VMEM((4, t, 1), F32), pltpu.VMEM((4, t, 1), F32),
                        pltpu.VMEM((4, t, DIFF_V_DIM), F32)],
    )
    assert t >= REL_MAX_DIST, "far blocks must lie entirely in the last distance bucket"
    return pl.pallas_call(
        functools.partial(_diff_prompt_kernel, lam_init=lam_init),
        out_shape=jax.ShapeDtypeStruct((s, DIFF_OUT_W), BF16),
        grid_spec=grid_spec,
        compiler_params=_params("arbitrary", "arbitrary"),
        name="diff_attn_prompt",
    )(qi_arr, kj_arr, u, u, u, bias, far, lam_params, subln.reshape(1, DIFF_V_DIM))


def _diff_sample_kernel(pt_ref, q_ref, kn_ref, vn_ref, bias_ref, bias0_ref, lam_ref, subln_ref, *rest,
                        pages, lam_init):
    k_refs = rest[:pages]
    v_refs = rest[pages:2 * pages]
    o_ref, m_sc, l_sc, acc_sc = rest[2 * pages:]
    c = pl.program_id(1)
    scale = DIFF_HEAD_DIM ** -0.5
    page = k_refs[0].shape[0] // 8
    kw = DIFF_K_W
    lane = lax.broadcasted_iota(jnp.int32, (DIFF_MAPS, kw), 1)
    row = lax.broadcasted_iota(jnp.int32, (DIFF_MAPS, kw), 0)
    q_bd = jnp.where((lane // DIFF_HEAD_DIM) == (row // DIFF_GROUP), q_ref[...], 0.0).astype(BF16)

    @pl.when(c == 0)
    def _():
        kn = kn_ref[...].astype(BF16).astype(F32)
        s0 = jnp.sum(q_bd.astype(F32) * kn, axis=-1, keepdims=True) * scale + bias0_ref[...]
        m_sc[...] = s0
        l_sc[...] = jnp.ones(l_sc.shape, F32)
        acc_sc[...] = jnp.broadcast_to(vn_ref[...].astype(BF16).astype(F32), acc_sc.shape)

    def matrix(ref, starts):
        return jnp.concatenate([ref[pl.ds(r, page, stride=8), :] for r in starts], axis=1).astype(BF16)

    k_rows = list(range(8))
    v_rows = [half * DIFF_KV_HEADS + hh for hh in range(DIFF_KV_HEADS) for half in range(2)]
    s = jnp.concatenate([_dot_nt(q_bd, matrix(k_refs[i], k_rows)) for i in range(pages)], axis=1)
    s = s * scale + bias_ref[:, pl.ds(pl.multiple_of(c * (pages * page), LANES), pages * page)]
    m_old = m_sc[...]
    m_new = jnp.maximum(m_old, jnp.max(s, axis=-1, keepdims=True))
    p = jnp.exp(s - m_new)
    alpha = jnp.exp(m_old - m_new)
    l_sc[...] = alpha * l_sc[...] + jnp.sum(p, axis=-1, keepdims=True)
    pb = p.astype(BF16)
    pv = _dot(pb[:, 0:page], matrix(v_refs[0], v_rows))
    for i in range(1, pages):
        pv = pv + _dot(pb[:, i * page:(i + 1) * page], matrix(v_refs[i], v_rows))
    acc_sc[...] = alpha * acc_sc[...] + pv
    m_sc[...] = m_new

    @pl.when(c == pl.num_programs(1) - 1)
    def _():
        lam = _lambda_value(lam_ref, lam_init)
        o = acc_sc[...] / l_sc[...]
        for h in range(DIFF_KV_HEADS):
            for g in range(DIFF_GROUP):
                r0 = h * 4 + g
                r1 = h * 4 + 2 + g
                cols = slice(h * DIFF_V_DIM, (h + 1) * DIFF_V_DIM)
                res = _diff_finish(o[r0:r0 + 1, cols], o[r1:r1 + 1, cols], lam, subln_ref[...], lam_init)
                o_ref[:, pl.ds((h * DIFF_GROUP + g) * DIFF_V_DIM, DIFF_V_DIM)] = res.astype(o_ref.dtype)


def _diff_sample(u, cache_k, cache_v, page_table, rel_table, lam_params, subln, lam_init, pages=16):
    db = u.shape[0]
    n_pages = page_table.shape[1]
    page = cache_k.shape[1] // 8
    pages = min(pages, n_pages)
    assert n_pages % pages == 0
    past = n_pages * page
    order = np.array([h * 4 + g * 2 + mp for h in range(DIFF_KV_HEADS) for mp in range(2) for g in range(DIFF_GROUP)])
    q = u[:, :DIFF_Q_W].reshape(db, DIFF_KV_HEADS, DIFF_GROUP, 2, DIFF_HEAD_DIM)
    q = q.transpose(0, 1, 3, 2, 4).reshape(db, DIFF_MAPS, DIFF_HEAD_DIM)
    q_t = jnp.tile(q, (1, 1, DIFF_K_W // DIFF_HEAD_DIM))
    k_new = u[:, DIFF_Q_W:DIFF_Q_W + DIFF_K_W].reshape(db, 1, DIFF_K_W)
    v_new = u[:, DIFF_Q_W + DIFF_K_W:].reshape(db, 1, DIFF_V_W)
    table = rel_table[:, order]
    bias = _bias_tiles(table, 8, past, past, 0, ANY_DIST)[:, 0]
    bias0 = table[0].reshape(DIFF_MAPS, 1)
    pt_flat = page_table.reshape(-1)

    def page_spec(i):
        return pl.BlockSpec((None, page * 8, LANES),
                            lambda b, c, pt: (pt[b * n_pages + c * pages + i], 0, 0))

    const2 = lambda b, c, pt: (0, 0)
    grid_spec = pltpu.PrefetchScalarGridSpec(
        num_scalar_prefetch=1,
        grid=(db, n_pages // pages),
        in_specs=[pl.BlockSpec((None, DIFF_MAPS, DIFF_K_W), lambda b, c, pt: (b, 0, 0)),
                  pl.BlockSpec((None, 1, DIFF_K_W), lambda b, c, pt: (b, 0, 0)),
                  pl.BlockSpec((None, 1, DIFF_V_W), lambda b, c, pt: (b, 0, 0)),
                  pl.BlockSpec(bias.shape, const2),
                  pl.BlockSpec((DIFF_MAPS, 1), const2),
                  pl.BlockSpec((4, DIFF_HEAD_DIM), const2),
                  pl.BlockSpec((1, DIFF_V_DIM), const2)]
                 + [page_spec(i) for i in range(pages)] * 2,
        out_specs=pl.BlockSpec((None, 1, DIFF_OUT_W), lambda b, c, pt: (b, 0, 0)),
        scratch_shapes=[pltpu.VMEM((DIFF_MAPS, 1), F32), pltpu.VMEM((DIFF_MAPS, 1), F32),
                        pltpu.VMEM((DIFF_MAPS, DIFF_V_W), F32)],
    )
    o = pl.pallas_call(
        functools.partial(_diff_sample_kernel, pages=pages, lam_init=lam_init),
        out_shape=jax.ShapeDtypeStruct((db, 1, DIFF_OUT_W), BF16),
        grid_spec=grid_spec,
        compiler_params=_params("arbitrary", "arbitrary"),
        name="diff_attn_sample",
    )(pt_flat, q_t, k_new, v_new, bias, bias0, lam_params, subln.reshape(1, DIFF_V_DIM),
      *([cache_k] * pages), *([cache_v] * pages))
    return o.reshape(db, DIFF_OUT_W)


def _routing_tables(idx, tm):
    m = idx.shape[0]
    e_flat = idx.reshape(-1)
    onehot = (e_flat[:, None] == jnp.arange(N_EXPERTS)[None, :]).astype(jnp.int32)
    rank = jnp.sum((jnp.cumsum(onehot, axis=0) - onehot) * onehot, axis=1)
    counts = jnp.sum(onehot, axis=0)
    tiles = (counts + tm - 1) // tm
    tile_end = jnp.cumsum(tiles)
    tile_start = tile_end - tiles
    pos = tile_start[e_flat] * tm + rank
    n_tiles = (TOP_K * m + tm - 1) // tm + N_EXPERTS
    t_ids = jnp.arange(n_tiles)
    tile_valid = (t_ids < tile_end[-1]).astype(jnp.int32)
    last_used = jnp.max(jnp.where(counts > 0, jnp.arange(N_EXPERTS), 0))
    tile_expert = jnp.sum((t_ids[:, None] >= tile_end[None, :]).astype(jnp.int32), axis=1)
    tile_expert = jnp.minimum(tile_expert, last_used).astype(jnp.int32)
    src = jnp.zeros((n_tiles * tm,), jnp.int32).at[pos].set(jnp.arange(TOP_K * m, dtype=jnp.int32) // TOP_K)
    pos = pos.reshape(m, TOP_K).astype(jnp.int32)
    return pos[:, 0], pos[:, 1], src, tile_expert, tile_valid


def kernel(x_prompt, x_sample, mem_prompt, state_pool, state_swa_k, state_swa_v, cache_diff_k, cache_diff_v,
           cache_mem_k, cache_mem_v, page_table, rel_bias_table, norm_gain, final_gain, w_in_even, pool_map,
           pool_scale, attn_sinks, w_out_even, w_qkv_odd, diff_lambda, diff_subln, w_out_odd, w_mem_q, w_mem_k,
           w_mem_v, w_mem_o, w_ffn_gu, w_ffn_down, w_router, w_exp_gu, w_exp_down):
    _, s, d = x_prompt.shape
    db = x_sample.shape[0]
    n_phys, page = cache_diff_k.shape[:2]
    hp = x_prompt.reshape(s, d)
    hs = x_sample.reshape(db, d)
    mem = mem_prompt.reshape(-1, d)
    ml = mem.shape[0]
    bias_swa = _swa_bias(rel_bias_table)
    n_layers = cache_mem_k.shape[0]
    cmk = cache_mem_k.reshape(n_layers, db, -1, MEM_HEAD_DIM)
    cmv = cache_mem_v.reshape(n_layers, db, -1, MEM_HEAD_DIM)

    def mem_block(hp, hs, layer):
        mk = _matmul(mem, w_mem_k[layer])
        mv = _matmul(mem, w_mem_v[layer])
        g_mem = norm_gain[layer, 1]
        qp = _matmul(_rmsnorm(hp, g_mem), w_mem_q[layer])
        hp = _matmul(_mem_prompt(qp, mk, mv), w_mem_o[layer], resid=hp)
        qs = _matmul(_rmsnorm(hs, g_mem), w_mem_q[layer])
        os_ = _mem_sample(qs, cmk, cmv, layer)
        hs = _matmul(os_, w_mem_o[layer], resid=hs)
        return hp, hs, mk, mv

    g_mix = norm_gain[0, 0]
    up = _matmul(_rmsnorm(hp, g_mix), w_in_even[0])
    us = _matmul(_rmsnorm(hs, g_mix), w_in_even[0])
    a_p = _pool_prompt(up, pool_map[0], pool_scale[0])
    b_p = _swa_prompt(up, bias_swa, attn_sinks[0])
    hp = _matmul(jnp.concatenate([a_p, b_p], axis=1), w_out_even[0], resid=hp)

    o1 = POOL_WIDTH
    o2 = o1 + SWA_Q_W
    o3 = o2 + SWA_KV_W
    a_s = _pool_sample(state_pool[0], us, pool_map[0], pool_scale[0])
    k_old = state_swa_k[0].reshape(db, WINDOW, SWA_KV_W)
    v_old = state_swa_v[0].reshape(db, WINDOW, SWA_KV_W)
    k_state = jnp.concatenate([k_old[:, 1:], us[:, None, o2:o3]], axis=1)
    v_state = jnp.concatenate([v_old[:, 1:], us[:, None, o3:]], axis=1)
    b_s = _swa_sample(us[:, o1:o2].reshape(db, SWA_HEADS, SWA_HEAD_DIM), k_state, v_state,
                      k_old[:, :1], v_old[:, :1], bias_swa, attn_sinks[0])
    hs = _matmul(jnp.concatenate([a_s, b_s], axis=1), w_out_even[0], resid=hs)

    pool_p = up[s - POOL_STATE:, :o1].reshape(1, 1, POOL_STATE, POOL_WIDTH)
    pool_s = jnp.concatenate([state_pool[0][:, 1:], us[:, None, :o1]], axis=1)[None]
    swk_p = up[s - WINDOW:, o2:o3].reshape(1, 1, WINDOW, SWA_KV_HEADS, SWA_HEAD_DIM)
    swv_p = up[s - WINDOW:, o3:].reshape(1, 1, WINDOW, SWA_KV_HEADS, SWA_HEAD_DIM)
    swk_s = k_state.reshape(1, db, WINDOW, SWA_KV_HEADS, SWA_HEAD_DIM)
    swv_s = v_state.reshape(1, db, WINDOW, SWA_KV_HEADS, SWA_HEAD_DIM)

    hp, hs, mk0, mv0 = mem_block(hp, hs, 0)

    g_ffn = norm_gain[0, 2]
    hp = _ffn(_rmsnorm(hp, g_ffn), w_ffn_gu[0], w_ffn_down[0], hp)
    hs = _ffn(_rmsnorm(hs, g_ffn), w_ffn_gu[0], w_ffn_down[0], hs)

    lam_init = 0.8 - 0.6 * math.exp(-0.3 * 1)
    g_mix = norm_gain[1, 0]
    up = _matmul(_rmsnorm(hp, g_mix), w_qkv_odd[0])
    us = _matmul(_rmsnorm(hs, g_mix), w_qkv_odd[0])
    o_p = _diff_prompt(up, rel_bias_table, diff_lambda[0], diff_subln[0], lam_init)
    hp = _matmul(o_p, w_out_odd[0], resid=hp)
    ck = cache_diff_k.reshape(n_phys, page * 8, LANES)
    cv = cache_diff_v.reshape(n_phys, page, DIFF_KV_HEADS, 2, LANES).transpose(0, 1, 3, 2, 4)
    cv = cv.reshape(n_phys, page * 8, LANES)
    o_s = _diff_sample(us, ck, cv, page_table, rel_bias_table, diff_lambda[0], diff_subln[0], lam_init)
    hs = _matmul(o_s, w_out_odd[0], resid=hs)

    q_end = DIFF_Q_W
    k_end = q_end + DIFF_K_W
    dk_p = up[:, q_end:k_end].reshape(1, s, 1, DIFF_KV_HEADS, 2, DIFF_HEAD_DIM)
    dv_p = up[:, k_end:].reshape(1, s, 1, DIFF_KV_HEADS, DIFF_V_DIM)
    dk_s = us[:, q_end:k_end].reshape(db, 1, 1, DIFF_KV_HEADS, 2, DIFF_HEAD_DIM)
    dv_s = us[:, k_end:].reshape(db, 1, 1, DIFF_KV_HEADS, DIFF_V_DIM)

    hp, hs, mk1, mv1 = mem_block(hp, hs, 1)

    h_all = jnp.concatenate([hp, hs], axis=0)
    x_all = _rmsnorm(h_all, norm_gain[1, 2], out_dtype=F32)
    idx, gates = _router(x_all, w_router[0])
    tm = 768
    pos1, pos2, src, tile_expert, tile_valid = _routing_tables(idx[:, :TOP_K], tm)
    x_sorted = _gather_rows(x_all, src)
    y_sorted = _moe_experts(x_sorted, tile_expert, tile_valid, w_exp_gu[0], w_exp_down[0], tm)
    y_all = _moe_combine_norm(y_sorted, pos1, pos2, gates, h_all, final_gain)

    mem_shape = (1, ml, MEM_HEADS, MEM_HEAD_DIM)
    return (y_all[:s].reshape(1, s, d), y_all[s:].reshape(db, 1, d),
            pool_p, pool_s, swk_p, swv_p, swk_s, swv_s,
            dk_p, dv_p, dk_s, dv_s,
            jnp.stack([mk0.reshape(mem_shape), mk1.reshape(mem_shape)]),
            jnp.stack([mv0.reshape(mem_shape), mv1.reshape(mem_shape)]))
```
